```python
import math
import jax, jax.numpy as jnp
from jax import lax
import numpy as np

D_MODEL = 1024
BATCH = 2
SEQ = 8192
DEPTH = 4
DEC_BATCH = 32
DEC_SEQ = 1
PAST_LEN = 8192
PAGE_SIZE = 128

N_MIXERS = 2
N_MLSTM_LAYERS = (DEPTH + 1) // 2
N_MOBA_LAYERS = DEPTH // 2
ML_HEADS = 8
ML_DV = D_MODEL // ML_HEADS
ML_DQK = ML_DV // 2
ML_QK_W = ML_HEADS * ML_DQK
ML_V_W = ML_HEADS * ML_DV
ML_PROJ = 2 * ML_QK_W + 2 * ML_V_W + 2 * ML_HEADS
ML_CHUNK = 64
AT_HEADS = 16
AT_DH = D_MODEL // AT_HEADS
MOBA_BLOCK = 256
MOBA_TOPK = 3
MOBA_Q_BLOCK = 32
NUM_BUCKETS = 32
MAX_DISTANCE = 4096
D_FF = -(-(8 * D_MODEL) // (3 * 256)) * 256
EPS = 1e-6

kernel_name = "mlstm_moba_interleaved_decoder_step"


def rmsnorm(x, g):
    x32 = x.astype(jnp.float32)
    y = x32 * lax.rsqrt(jnp.mean(x32 * x32, axis=-1, keepdims=True) + EPS) * g.astype(jnp.float32)
    return y.astype(x.dtype)


def swiglu(hn, w_in, w_out):
    gu = hn @ w_in
    g, u = gu[..., :D_FF], gu[..., D_FF:]
    return (jax.nn.silu(g) * u) @ w_out


def rel_bucket(dist):
    n = jnp.maximum(dist, 0)
    max_exact = NUM_BUCKETS // 2
    nf = jnp.maximum(n, 1).astype(jnp.float32)
    large = max_exact + (jnp.log(nf / max_exact) / math.log(MAX_DISTANCE / max_exact)
                         * (NUM_BUCKETS - max_exact)).astype(jnp.int32)
    return jnp.where(n < max_exact, n, jnp.minimum(large, NUM_BUCKETS - 1))


def mlstm_mixer(hn, C0, n0, m0, w_in, b_gates, g_head, w_out):
    B, S, _ = hn.shape
    proj = hn @ w_in
    q = proj[..., :ML_QK_W].reshape(B, S, ML_HEADS, ML_DQK)
    k = proj[..., ML_QK_W:2 * ML_QK_W].reshape(B, S, ML_HEADS, ML_DQK) * (ML_DQK ** -0.5)
    v = proj[..., 2 * ML_QK_W:2 * ML_QK_W + ML_V_W].reshape(B, S, ML_HEADS, ML_DV)
    o = proj[..., 2 * ML_QK_W + ML_V_W:2 * ML_QK_W + 2 * ML_V_W]
    gates = (proj[..., 2 * ML_QK_W + 2 * ML_V_W:] + b_gates).astype(jnp.float32)
    ig = gates[..., :ML_HEADS]
    logf = jax.nn.log_sigmoid(gates[..., ML_HEADS:])

    L = math.gcd(S, ML_CHUNK)
    nc = S // L

    def to_chunks(a):
        return a.astype(jnp.float32).reshape(B, nc, L, ML_HEADS, -1).transpose(1, 0, 3, 2, 4)

    def gate_chunks(a):
        return a.reshape(B, nc, L, ML_HEADS).transpose(1, 0, 3, 2)

    causal = jnp.tril(jnp.ones((L, L), dtype=bool))

    def step(carry, xs):
        C, n, m = carry
        qc, kc, vc, ic, fc = xs
        F = jnp.cumsum(fc, axis=-1)
        Dm = F[..., :, None] - F[..., None, :] + ic[..., None, :]
        Dm = jnp.where(causal, Dm, -jnp.inf)
        m_inter = m[..., None] + F
        m_t = jnp.maximum(m_inter, jnp.max(Dm, axis=-1))
        W = jnp.exp(Dm - m_t[..., None])
        a = jnp.exp(m_inter - m_t)
        Sqk = jnp.einsum('bhtd,bhsd->bhts', qc, kc) * W
        num = a[..., None] * jnp.einsum('bhtd,bhde->bhte', qc, C) + jnp.einsum('bhts,bhse->bhte', Sqk, vc)
        den = a * jnp.einsum('bhtd,bhd->bht', qc, n) + jnp.sum(Sqk, axis=-1)
        h = num / jnp.maximum(jnp.abs(den), jnp.exp(-m_t))[..., None]
        w_last, a_last = W[..., -1, :], a[..., -1]
        C_new = a_last[..., None, None] * C + jnp.einsum('bhs,bhsd,bhse->bhde', w_last, kc, vc)
        n_new = a_last[..., None] * n + jnp.einsum('bhs,bhsd->bhd', w_last, kc)
        return (C_new, n_new, m_t[..., -1]), h

    xs = (to_chunks(q), to_chunks(k), to_chunks(v), gate_chunks(ig), gate_chunks(logf))
    init = (C0.astype(jnp.float32), n0.astype(jnp.float32), m0.astype(jnp.float32))
    (C, n, m), hs = lax.scan(step, init, xs)
    hs = hs.transpose(1, 0, 3, 2, 4).reshape(B, S, ML_HEADS, ML_DV)
    hs = hs * lax.rsqrt(jnp.mean(hs * hs, axis=-1, keepdims=True) + EPS) \
        * g_head.astype(jnp.float32).reshape(ML_HEADS, ML_DV)
    out = hs.reshape(B, S, ML_V_W) * jax.nn.sigmoid(o.astype(jnp.float32))
    return out.astype(hn.dtype) @ w_out, C, n, m


def moba_core(q, pos_q, kmeans, gather_kv, k_loc, v_loc, pos_loc, rel_table):
    B, Q, H, Dh = q.shape
    q = q.astype(jnp.float32)
    nbc = kmeans.shape[1]
    scale = Dh ** -0.5
    heads = jnp.arange(H)
    own_start = (pos_q // MOBA_BLOCK) * MOBA_BLOCK
    dist_loc = pos_q[:, None] - pos_loc[None, :]
    mask_loc = (dist_loc >= 0) & (pos_loc[None, :] >= own_start[:, None])
    bias_loc = jnp.transpose(rel_table[rel_bucket(dist_loc)], (0, 2, 1))[None]
    s_loc = jnp.einsum('bqhd,blhd->bqhl', q, k_loc) * scale + bias_loc
    s_loc = jnp.where(mask_loc[None, :, None, :], s_loc, -jnp.inf)
    n_sel = min(MOBA_TOPK, nbc)
    if n_sel == 0:
        p = jax.nn.softmax(s_loc, axis=-1)
        return jnp.einsum('bqhl,blhd->bqhd', p, v_loc)
    n_avail = jnp.minimum(pos_q // MOBA_BLOCK, nbc)
    gate = jnp.einsum('bqhd,bjhd->bqhj', q, kmeans.astype(jnp.float32))
    gate = jnp.where(jnp.arange(nbc)[None, None, None, :] < n_avail[None, :, None, None], gate, -jnp.inf)
    _, sel = lax.top_k(gate, n_sel)
    valid = jnp.arange(n_sel)[None, :] < n_avail[:, None]
    k_sel, v_sel = gather_kv(sel)
    key_pos = sel[..., None] * MOBA_BLOCK + jnp.arange(MOBA_BLOCK)
    bias_sel = rel_table[rel_bucket(pos_q[None, :, None, None, None] - key_pos), heads[None, None, :, None, None]]
    s_sel = jnp.einsum('bqhd,bqhkrd->bqhkr', q, k_sel) * scale + bias_sel
    s_sel = jnp.where(valid[None, :, None, :, None], s_sel, -jnp.inf).reshape(B, Q, H, n_sel * MOBA_BLOCK)
    p = jax.nn.softmax(jnp.concatenate([s_sel, s_loc], axis=-1).astype(jnp.float32), axis=-1)
    p_sel = p[..., :n_sel * MOBA_BLOCK].reshape(B, Q, H, n_sel, MOBA_BLOCK)
    p_loc = p[..., n_sel * MOBA_BLOCK:]
    return jnp.einsum('bqhkr,bqhkrd->bqhd', p_sel, v_sel) + jnp.einsum('bqhl,blhd->bqhd', p_loc, v_loc)


def moba_prompt(hn, w_qkv, w_out, rel_table):
    B, S, _ = hn.shape
    qkv = (hn @ w_qkv).reshape(B, S, 3, AT_HEADS, AT_DH)
    q, k, v = qkv[:, :, 0], qkv[:, :, 1], qkv[:, :, 2]
    nb = -(-S // MOBA_BLOCK)
    pad = nb * MOBA_BLOCK - S
    k_pad = jnp.pad(k, ((0, 0), (0, pad), (0, 0), (0, 0)))
    v_pad = jnp.pad(v, ((0, 0), (0, pad), (0, 0), (0, 0)))
    kmeans = jnp.mean(k_pad.reshape(B, nb, MOBA_BLOCK, AT_HEADS, AT_DH), axis=2, dtype=jnp.float32)
    kb = k_pad.reshape(B, nb, MOBA_BLOCK, AT_HEADS, AT_DH).transpose(0, 3, 1, 2, 4)
    vb = v_pad.reshape(B, nb, MOBA_BLOCK, AT_HEADS, AT_DH).transpose(0, 3, 1, 2, 4)
    b_idx = jnp.arange(B)[:, None, None, None]
    h_idx = jnp.arange(AT_HEADS)[None, None, :, None]

    def gather_kv(sel):
        return kb[b_idx, h_idx, sel], vb[b_idx, h_idx, sel]

    nqc = S // MOBA_Q_BLOCK
    q_chunks = q.reshape(B, nqc, MOBA_Q_BLOCK, AT_HEADS, AT_DH).transpose(1, 0, 2, 3, 4)

    def one(args):
        c, qq = args
        pos_q = c * MOBA_Q_BLOCK + jnp.arange(MOBA_Q_BLOCK, dtype=jnp.int32)
        start = (c * MOBA_Q_BLOCK // MOBA_BLOCK) * MOBA_BLOCK
        k_loc = lax.dynamic_slice_in_dim(k_pad, start, MOBA_BLOCK, axis=1)
        v_loc = lax.dynamic_slice_in_dim(v_pad, start, MOBA_BLOCK, axis=1)
        pos_loc = start + jnp.arange(MOBA_BLOCK, dtype=jnp.int32)
        return moba_core(qq, pos_q, kmeans, gather_kv, k_loc, v_loc, pos_loc, rel_table)

    o = lax.map(one, (jnp.arange(nqc, dtype=jnp.int32), q_chunks))
    o = o.transpose(1, 0, 2, 3, 4).reshape(B, S, AT_HEADS * AT_DH)
    return o.astype(hn.dtype) @ w_out, k, v


def moba_sample(hn, cache_k, cache_v, layer, page_table, w_qkv, w_out, rel_table):
    Bd, T, _ = hn.shape
    qkv = (hn @ w_qkv).reshape(Bd, T, 3, AT_HEADS, AT_DH)
    q, k, v = qkv[:, :, 0], qkv[:, :, 1], qkv[:, :, 2]
    n_pages = page_table.shape[1]
    past = n_pages * PAGE_SIZE
    ppb = MOBA_BLOCK // PAGE_SIZE
    nbc = past // MOBA_BLOCK
    tail = past - nbc * MOBA_BLOCK
    assert T <= MOBA_BLOCK - tail
    pt_full = page_table[:, :nbc * ppb]
    k_full = cache_k[layer, pt_full]
    kmeans = jnp.mean(k_full.reshape(Bd, nbc, MOBA_BLOCK, AT_HEADS, AT_DH), axis=2, dtype=jnp.float32)
    pt_tail = page_table[:, nbc * ppb:]
    k_tail = cache_k[layer, pt_tail].reshape(Bd, tail, AT_HEADS, AT_DH)
    v_tail = cache_v[layer, pt_tail].reshape(Bd, tail, AT_HEADS, AT_DH)
    k_loc = jnp.concatenate([k_tail.astype(k.dtype), k], axis=1)
    v_loc = jnp.concatenate([v_tail.astype(v.dtype), v], axis=1)
    pos_loc = nbc * MOBA_BLOCK + jnp.arange(tail + T, dtype=jnp.int32)
    pos_q = past + jnp.arange(T, dtype=jnp.int32)
    b_idx = jnp.arange(Bd)[:, None, None, None, None]
    h_idx = jnp.arange(AT_HEADS)[None, None, :, None, None, None]

    def gather_kv(sel):
        lp = sel[..., None] * ppb + jnp.arange(ppb)
        phys = page_table[b_idx, lp][..., None]
        rows = jnp.arange(PAGE_SIZE)
        kk = cache_k[layer, phys, rows, h_idx]
        vv = cache_v[layer, phys, rows, h_idx]
        shp = sel.shape + (MOBA_BLOCK, AT_DH)
        return kk.reshape(shp), vv.reshape(shp)

    o = moba_core(q, pos_q, kmeans, gather_kv, k_loc, v_loc, pos_loc, rel_table)
    o = o.reshape(Bd, T, AT_HEADS * AT_DH)
    return o.astype(hn.dtype) @ w_out, k, v


def setup_inputs(seed: int = 0) -> dict:
    key = jax.random.key(seed)
    ks = jax.random.split(key, 24)
    f32 = jnp.float32
    n_pages = PAST_LEN // PAGE_SIZE
    n_used = DEC_BATCH * n_pages
    n_pool = n_used + n_used // 4
    nrm = lambda k, shp, s: jax.random.normal(k, shp, f32) * s
    x_prompt = nrm(ks[0], (BATCH, SEQ, D_MODEL), 1.0)
    x_sample = nrm(ks[1], (DEC_BATCH, DEC_SEQ, D_MODEL), 1.0)
    cache_k = nrm(ks[2], (N_MOBA_LAYERS, n_pool, PAGE_SIZE, AT_HEADS, AT_DH), 1.0)
    cache_v = nrm(ks[3], (N_MOBA_LAYERS, n_pool, PAGE_SIZE, AT_HEADS, AT_DH), 1.0)
    state_C = nrm(ks[4], (N_MLSTM_LAYERS, DEC_BATCH, ML_HEADS, ML_DQK, ML_DV), 0.3)
    state_n = nrm(ks[5], (N_MLSTM_LAYERS, DEC_BATCH, ML_HEADS, ML_DQK), 0.5)
    state_m = nrm(ks[6], (N_MLSTM_LAYERS, DEC_BATCH, ML_HEADS), 0.5)
    page_table = jax.random.permutation(ks[7], n_pool)[:n_used].reshape(DEC_BATCH, n_pages).astype(jnp.int32)
    norm_mix = 1.0 + nrm(ks[8], (DEPTH, D_MODEL), 0.02)
    norm_ffn = 1.0 + nrm(ks[9], (DEPTH, D_MODEL), 0.02)
    norm_final = 1.0 + nrm(ks[10], (D_MODEL,), 0.02)
    rel_bias = nrm(ks[11], (NUM_BUCKETS, AT_HEADS), 0.5)
    w_ml_in = nrm(ks[12], (N_MLSTM_LAYERS, D_MODEL, ML_PROJ), D_MODEL ** -0.5)
    b_i = nrm(ks[13], (N_MLSTM_LAYERS, ML_HEADS), 0.1)
    b_f = jnp.linspace(3.0, 6.0, ML_HEADS, dtype=f32)[None] + nrm(ks[14], (N_MLSTM_LAYERS, ML_HEADS), 0.1)
    b_ml_gates = jnp.concatenate([b_i, b_f], axis=-1)
    g_ml_head = 1.0 + nrm(ks[15], (N_MLSTM_LAYERS, ML_V_W), 0.02)
    w_ml_out = nrm(ks[16], (N_MLSTM_LAYERS, ML_V_W, D_MODEL), ML_V_W ** -0.5)
    w_attn_qkv = nrm(ks[17], (N_MOBA_LAYERS, D_MODEL, 3 * AT_HEADS * AT_DH), D_MODEL ** -0.5)
    w_attn_out = nrm(ks[18], (N_MOBA_LAYERS, AT_HEADS * AT_DH, D_MODEL), (AT_HEADS * AT_DH) ** -0.5)
    w_ffn_in = nrm(ks[19], (DEPTH, D_MODEL, 2 * D_FF), D_MODEL ** -0.5)
    w_ffn_out = nrm(ks[20], (DEPTH, D_FF, D_MODEL), D_FF ** -0.5)
    return {"x_prompt": x_prompt, "x_sample": x_sample, "cache_k": cache_k, "cache_v": cache_v,
            "state_C": state_C, "state_n": state_n, "state_m": state_m, "page_table": page_table,
            "norm_mix": norm_mix, "norm_ffn": norm_ffn, "norm_final": norm_final, "rel_bias": rel_bias,
            "w_ml_in": w_ml_in, "b_ml_gates": b_ml_gates, "g_ml_head": g_ml_head, "w_ml_out": w_ml_out,
            "w_attn_qkv": w_attn_qkv, "w_attn_out": w_attn_out, "w_ffn_in": w_ffn_in, "w_ffn_out": w_ffn_out}


def reference(x_prompt, x_sample, cache_k, cache_v, state_C, state_n, state_m, page_table,
              norm_mix, norm_ffn, norm_final, rel_bias, w_ml_in, b_ml_gates, g_ml_head, w_ml_out,
              w_attn_qkv, w_attn_out, w_ffn_in, w_ffn_out):
    B = x_prompt.shape[0]
    hp, hs = x_prompt, x_sample
    k_p, v_p, k_s, v_s = [], [], [], []
    C_p, n_p, m_p, C_s, n_s, m_s = [], [], [], [], [], []
    for layer in range(DEPTH):
        j = layer // N_MIXERS
        hpn = rmsnorm(hp, norm_mix[layer])
        hsn = rmsnorm(hs, norm_mix[layer])
        if layer % N_MIXERS == 0:
            zC = jnp.zeros((B, ML_HEADS, ML_DQK, ML_DV), jnp.float32)
            zn = jnp.zeros((B, ML_HEADS, ML_DQK), jnp.float32)
            zm = jnp.zeros((B, ML_HEADS), jnp.float32)
            yp, Cp, np_, mp = mlstm_mixer(hpn, zC, zn, zm, w_ml_in[j], b_ml_gates[j], g_ml_head[j], w_ml_out[j])
            ys, Cs, ns, ms = mlstm_mixer(hsn, state_C[j], state_n[j], state_m[j],
                                         w_ml_in[j], b_ml_gates[j], g_ml_head[j], w_ml_out[j])
            C_p.append(Cp); n_p.append(np_); m_p.append(mp)
            C_s.append(Cs); n_s.append(ns); m_s.append(ms)
        else:
            yp, kp, vp = moba_prompt(hpn, w_attn_qkv[j], w_attn_out[j], rel_bias)
            ys, ks_, vs_ = moba_sample(hsn, cache_k, cache_v, j, page_table, w_attn_qkv[j], w_attn_out[j], rel_bias)
            k_p.append(kp); v_p.append(vp); k_s.append(ks_); v_s.append(vs_)
        hp = hp + yp
        hs = hs + ys
        hp = hp + swiglu(rmsnorm(hp, norm_ffn[layer]), w_ffn_in[layer], w_ffn_out[layer])
        hs = hs + swiglu(rmsnorm(hs, norm_ffn[layer]), w_ffn_in[layer], w_ffn_out[layer])
    y_prompt = rmsnorm(hp, norm_final)
    y_sample = rmsnorm(hs, norm_final)
    return (y_prompt, y_sample,
            jnp.stack(k_p), jnp.stack(v_p), jnp.stack(k_s), jnp.stack(v_s),
            jnp.stack(C_p), jnp.stack(n_p), jnp.stack(m_p),
            jnp.stack(C_s), jnp.stack(n_s), jnp.stack(m_s))
```

```python
import functools
import math

import jax
import jax.numpy as jnp
from jax import lax
from jax.experimental import pallas as pl
from jax.experimental.pallas import tpu as pltpu

F32 = jnp.float32
BF16 = jnp.bfloat16

EPS = 1e-6
ML_HEADS = 8
ML_CHUNK = 256
MOBA_BLOCK = 256
MOBA_TOPK = 3
NUM_BUCKETS = 32
MAX_DISTANCE = 4096
NEG_BIG = -1e30
LANES = 128
VMEM_LIMIT = 56 * 1024 * 1024


def _dot(a, b):
    return jnp.dot(a, b, preferred_element_type=F32)


def _dot_nt(a, b):
    return lax.dot_general(a, b, (((1,), (1,)), ((), ())), preferred_element_type=F32)


def _split2(x):
    hi = x.astype(BF16)
    lo = (x - hi.astype(F32)).astype(BF16)
    return hi, lo


def _split3(x):
    hi = x.astype(BF16)
    r = x - hi.astype(F32)
    mid = r.astype(BF16)
    lo = (r - mid.astype(F32)).astype(BF16)
    return hi, mid, lo


def _dot_f32(a, b, dot=_dot):
    ah, al = _split2(a)
    bh, bl = _split2(b)
    return dot(ah, bh) + dot(ah, bl) + dot(al, bh)


def _norm_rows(x, g):
    return x * lax.rsqrt(jnp.mean(x * x, axis=-1, keepdims=True) + EPS) * g


def _log_sigmoid(x):
    return jnp.minimum(x, 0.0) - jnp.log1p(jnp.exp(-jnp.abs(x)))


def _rel_bucket(dist):
    n = jnp.maximum(dist, 0)
    max_exact = NUM_BUCKETS // 2
    nf = jnp.maximum(n, 1).astype(F32)
    large = max_exact + (jnp.log(nf / max_exact) / math.log(MAX_DISTANCE / max_exact)
                         * (NUM_BUCKETS - max_exact)).astype(jnp.int32)
    return jnp.where(n < max_exact, n, jnp.minimum(large, NUM_BUCKETS - 1))


def _bias_lookup(bucket, rel_ref, h):
    acc = jnp.full(bucket.shape, rel_ref[0, h], F32)
    for k in range(1, NUM_BUCKETS):
        acc = jnp.where(bucket == k, rel_ref[k, h], acc)
    return acc


def _params(*sem):
    return pltpu.CompilerParams(dimension_semantics=sem, vmem_limit_bytes=VMEM_LIMIT)


def _row_tile(m, target):
    t = min(m, target)
    assert m % t == 0, (m, t)
    return t


def _proj_kernel(x_ref, g_ref, w_ref, *o_refs):
    xb = _norm_rows(x_ref[...], g_ref[...]).astype(BF16)
    c0 = 0
    for o_ref in o_refs:
        n = o_ref.shape[1]
        for c in range(0, n, 512):
            w = min(512, n - c)
            o_ref[:, c:c + w] = _dot(xb, w_ref[:, c0 + c:c0 + c + w])
        c0 += n


def norm_proj(x, g, w, widths, tm=512):
    m, d = x.shape
    tm = _row_tile(m, tm)
    n = w.shape[1]
    assert sum(widths) == n
    return pl.pallas_call(
        _proj_kernel,
        grid=(m // tm,),
        in_specs=[pl.BlockSpec((tm, d), lambda i: (i, 0)),
                  pl.BlockSpec((1, d), lambda i: (0, 0)),
                  pl.BlockSpec((d, n), lambda i: (0, 0))],
        out_specs=[pl.BlockSpec((tm, wd), lambda i: (i, 0)) for wd in widths],
        out_shape=[jax.ShapeDtypeStruct((m, wd), F32) for wd in widths],
        compiler_params=_params("parallel"),
        name="norm_proj",
    )(x, g.reshape(1, d), w)


def _ml_proj_kernel(x_ref, g_ref, w_ref, wkt_ref, wgt_ref, bg_ref, q_ref, v_ref, o_ref, kt_ref, gt_ref):
    xn = _norm_rows(x_ref[...], g_ref[...])
    xb = xn.astype(BF16)
    xlo = (xn - xb.astype(F32)).astype(BF16)
    c0 = 0
    for out in (q_ref, v_ref, o_ref):
        n = out.shape[1]
        for c in range(0, n, 512):
            out[:, c:c + 512] = _dot(xb, w_ref[:, c0 + c:c0 + c + 512])
        c0 += n
    kt_ref[...] = _dot_nt(wkt_ref[...], xb)
    gh, gl = _split2(wgt_ref[...])
    gt_ref[...] = _dot_nt(gh, xb) + _dot_nt(gh, xlo) + _dot_nt(gl, xb) + bg_ref[...]


def ml_proj(x, g, w_main, wkt, wgt, bg, tm=512):
    m, d = x.shape
    tm = _row_tile(m, tm)
    qk = wkt.shape[0]
    vw = (w_main.shape[1] - qk) // 2
    ng = wgt.shape[0]
    return pl.pallas_call(
        _ml_proj_kernel,
        grid=(m // tm,),
        in_specs=[pl.BlockSpec((tm, d), lambda i: (i, 0)),
                  pl.BlockSpec((1, d), lambda i: (0, 0)),
                  pl.BlockSpec(w_main.shape, lambda i: (0, 0)),
                  pl.BlockSpec(wkt.shape, lambda i: (0, 0)),
                  pl.BlockSpec(wgt.shape, lambda i: (0, 0)),
                  pl.BlockSpec((ng, 1), lambda i: (0, 0))],
        out_specs=[pl.BlockSpec((tm, qk), lambda i: (i, 0)),
                   pl.BlockSpec((tm, vw), lambda i: (i, 0)),
                   pl.BlockSpec((tm, vw), lambda i: (i, 0)),
                   pl.BlockSpec((qk, tm), lambda i: (0, i)),
                   pl.BlockSpec((ng, tm), lambda i: (0, i))],
        out_shape=[jax.ShapeDtypeStruct((m, qk), F32),
                   jax.ShapeDtypeStruct((m, vw), F32),
                   jax.ShapeDtypeStruct((m, vw), F32),
                   jax.ShapeDtypeStruct((qk, m), F32),
                   jax.ShapeDtypeStruct((ng, m), F32)],
        compiler_params=_params("parallel"),
        name="ml_proj",
    )(x, g.reshape(1, d), w_main, wkt, wgt, bg.reshape(ng, 1))


def _out_proj_kernel(a_ref, w_ref, r_ref, o_ref):
    o_ref[...] = r_ref[...] + _dot(a_ref[...].astype(BF16), w_ref[...])


def out_proj(a, w, res, tm=1024):
    m, k = a.shape
    d = w.shape[1]
    tm = _row_tile(m, tm)
    return pl.pallas_call(
        _out_proj_kernel,
        grid=(m // tm,),
        in_specs=[pl.BlockSpec((tm, k), lambda i: (i, 0)),
                  pl.BlockSpec((k, d), lambda i: (0, 0)),
                  pl.BlockSpec((tm, d), lambda i: (i, 0))],
        out_specs=pl.BlockSpec((tm, d), lambda i: (i, 0)),
        out_shape=jax.ShapeDtypeStruct((m, d), F32),
        compiler_params=_params("parallel"),
        name="out_proj",
    )(a, w, res)


def _ffn_kernel(x_ref, g_ref, wg_ref, wu_ref, wo_ref, o_ref, xn_s, acc_s):
    f = pl.program_id(1)

    @pl.when(f == 0)
    def _():
        xn_s[...] = _norm_rows(x_ref[...], g_ref[...]).astype(BF16)
        acc_s[...] = jnp.zeros_like(acc_s)

    xb = xn_s[...]
    gp = _dot(xb, wg_ref[...])
    up = _dot(xb, wu_ref[...])
    act = (gp * jax.nn.sigmoid(gp) * up).astype(BF16)
    acc_s[...] += _dot(act, wo_ref[...])

    @pl.when(f == pl.num_programs(1) - 1)
    def _():
        o_ref[...] = x_ref[...] + acc_s[...]


def ffn(x, g, w_in, w_out, tm=1024, tf=256):
    m, d = x.shape
    dff = w_out.shape[0]
    tm = _row_tile(m, tm)
    nf = dff // tf
    assert dff % tf == 0
    return pl.pallas_call(
        _ffn_kernel,
        grid=(m // tm, nf),
        in_specs=[pl.BlockSpec((tm, d), lambda i, f: (i, 0)),
                  pl.BlockSpec((1, d), lambda i, f: (0, 0)),
                  pl.BlockSpec((d, tf), lambda i, f: (0, f)),
                  pl.BlockSpec((d, tf), lambda i, f: (0, nf + f)),
                  pl.BlockSpec((tf, d), lambda i, f: (f, 0))],
        out_specs=pl.BlockSpec((tm, d), lambda i, f: (i, 0)),
        out_shape=jax.ShapeDtypeStruct((m, d), F32),
        scratch_shapes=[pltpu.VMEM((tm, d), BF16), pltpu.VMEM((tm, d), F32)],
        compiler_params=_params("parallel", "arbitrary"),
        name="ffn",
    )(x, g.reshape(1, d), w_in, w_in, w_out)


def _rmsnorm_kernel(x_ref, g_ref, o_ref):
    o_ref[...] = _norm_rows(x_ref[...], g_ref[...])


def rmsnorm(x, g, tm=1024):
    m, d = x.shape
    tm = _row_tile(m, tm)
    return pl.pallas_call(
        _rmsnorm_kernel,
        grid=(m // tm,),
        in_specs=[pl.BlockSpec((tm, d), lambda i: (i, 0)), pl.BlockSpec((1, d), lambda i: (0, 0))],
        out_specs=pl.BlockSpec((tm, d), lambda i: (i, 0)),
        out_shape=jax.ShapeDtypeStruct((m, d), F32),
        compiler_params=_params("parallel"),
        name="rmsnorm",
    )(x, g.reshape(1, d))


def _mlstm_chunk_kernel(q_ref, v_ref, o_ref, kt_ref, gt_ref, gh_ref, hs_ref, caug_ref, m_ref, caug_s, m_s):
    c = pl.program_id(1)
    L = q_ref.shape[0]
    nh = caug_s.shape[0]
    dqk = caug_s.shape[1]
    dv = caug_s.shape[2] // 2

    @pl.when(c == 0)
    def _():
        caug_s[...] = jnp.zeros_like(caug_s)
        m_s[...] = jnp.zeros_like(m_s)

    row = lax.broadcasted_iota(jnp.int32, (L, L), 0)
    col = lax.broadcasted_iota(jnp.int32, (L, L), 1)
    causal = col <= row
    lower = jnp.where(causal, 1.0, 0.0).astype(BF16)
    upper = jnp.where(row <= col, 1.0, 0.0).astype(BF16)
    gt = gt_ref[...]
    ig = gt[0:nh, :]
    logf = _log_sigmoid(gt[nh:2 * nh, :])
    parts = _split3(logf)
    f_row = _dot(parts[0], upper) + _dot(parts[1], upper) + _dot(parts[2], upper)
    f_col = _dot_nt(lower, parts[0]) + _dot_nt(lower, parts[1]) + _dot_nt(lower, parts[2])
    a_rows = ig - f_row
    ones_col = jnp.where(lax.broadcasted_iota(jnp.int32, (L, dv), 1) == 0, 1.0, 0.0).astype(BF16)

    for h in range(nh):
        m_prev = m_s[h:h + 1, 0:1]
        a_row = a_rows[h:h + 1, :]
        amat = jnp.where(causal, a_row, -jnp.inf)
        g = jnp.maximum(jnp.max(amat, axis=1, keepdims=True), m_prev)
        w = jnp.exp(amat - g)
        qh = q_ref[:, h * dqk:(h + 1) * dqk].astype(BF16)
        kth = kt_ref[h * dqk:(h + 1) * dqk, :] * (dqk ** -0.5)
        p = (_dot(qh, kth.astype(BF16)) * w).astype(BF16)
        vaug = jnp.concatenate([v_ref[:, h * dv:(h + 1) * dv].astype(BF16), ones_col], axis=1)
        caug = caug_s[h]
        a_int = jnp.exp(m_prev - g)
        tot = a_int * _dot(qh, caug.astype(BF16)) + _dot(p, vaug)
        m_t = f_col[:, h:h + 1] + g
        hh = tot[:, :dv] / jnp.maximum(jnp.abs(tot[:, dv:dv + 1]), jnp.exp(-m_t))
        hn = hh * lax.rsqrt(jnp.mean(hh * hh, axis=-1, keepdims=True) + EPS) * gh_ref[:, h * dv:(h + 1) * dv]
        hs_ref[:, h * dv:(h + 1) * dv] = (hn * jax.nn.sigmoid(o_ref[:, h * dv:(h + 1) * dv])).astype(hs_ref.dtype)
        g_last = g[L - 1:L, :]
        w_last = jnp.exp(a_row - g_last)
        caug_s[h] = jnp.exp(m_prev - g_last) * caug + _dot((kth * w_last).astype(BF16), vaug)
        m_s[h:h + 1, :] = jnp.broadcast_to(f_row[h:h + 1, L - 1:L] + g_last, (1, m_s.shape[1]))

    @pl.when(c == pl.num_programs(1) - 1)
    def _():
        caug_ref[...] = caug_s[...]
        m_ref[...] = m_s[...]


def mlstm_prompt(q, v, o, kt, gt, g_head, batch):
    m, qkw = q.shape
    vw = v.shape[1]
    nh = gt.shape[0] // 2
    dqk, dv = qkw // nh, vw // nh
    s = m // batch
    L = math.gcd(s, ML_CHUNK)
    nc = s // L
    hs, caug, mm = pl.pallas_call(
        _mlstm_chunk_kernel,
        grid=(batch, nc),
        in_specs=[pl.BlockSpec((L, qkw), lambda b, c: (b * nc + c, 0)),
                  pl.BlockSpec((L, vw), lambda b, c: (b * nc + c, 0)),
                  pl.BlockSpec((L, vw), lambda b, c: (b * nc + c, 0)),
                  pl.BlockSpec((qkw, L), lambda b, c: (0, b * nc + c)),
                  pl.BlockSpec((2 * nh, L), lambda b, c: (0, b * nc + c)),
                  pl.BlockSpec((1, vw), lambda b, c: (0, 0))],
        out_specs=[pl.BlockSpec((L, vw), lambda b, c: (b * nc + c, 0)),
                   pl.BlockSpec((None, nh, dqk, 2 * dv), lambda b, c: (b, 0, 0, 0)),
                   pl.BlockSpec((None, nh, LANES), lambda b, c: (b, 0, 0))],
        out_shape=[jax.ShapeDtypeStruct((m, vw), BF16),
                   jax.ShapeDtypeStruct((batch, nh, dqk, 2 * dv), F32),
                   jax.ShapeDtypeStruct((batch, nh, LANES), F32)],
        scratch_shapes=[pltpu.VMEM((nh, dqk, 2 * dv), F32), pltpu.VMEM((nh, LANES), F32)],
        compiler_params=_params("parallel", "arbitrary"),
        name="mlstm_prompt",
    )(q, v, o, kt, gt, g_head.reshape(1, vw))
    return hs, caug[..., :dv], caug[..., dv], mm[..., 0]


def _mlstm_step_kernel(q_ref, k_ref, v_ref, o_ref, gt_ref, m_ref, gh_ref, c_ref, n_ref,
                       hs_ref, cn_ref, nn_ref, mn_ref):
    nh, dqk, dv = c_ref.shape
    scale = dqk ** -0.5
    q = q_ref[...]
    k = k_ref[...] * scale
    eye = jnp.where(lax.broadcasted_iota(jnp.int32, (dqk, dqk), 0)
                    == lax.broadcasted_iota(jnp.int32, (dqk, dqk), 1), 1.0, 0.0).astype(BF16)
    q3, k3 = _split3(q), _split3(k)
    q_cols = _dot_nt(eye, q3[0]) + _dot_nt(eye, q3[1]) + _dot_nt(eye, q3[2])
    k_cols = _dot_nt(eye, k3[0]) + _dot_nt(eye, k3[1]) + _dot_nt(eye, k3[2])
    gt = gt_ref[...]
    for h in range(nh):
        ig = gt[:, h:h + 1]
        logf = _log_sigmoid(gt[:, nh + h:nh + h + 1])
        m_inter = m_ref[:, h:h + 1] + logf
        m_t = jnp.maximum(m_inter, ig)
        w = jnp.exp(ig - m_t)
        a = jnp.exp(m_inter - m_t)
        q_row, k_row = q[h:h + 1, :], k[h:h + 1, :]
        q_col, k_col = q_cols[:, h:h + 1], k_cols[:, h:h + 1]
        v_row = v_ref[h:h + 1, :]
        c_h = c_ref[h]
        n_row = n_ref[h:h + 1, :]
        sqk = jnp.sum(q_row * k_row, axis=1, keepdims=True) * w
        num = a * jnp.sum(q_col * c_h, axis=0, keepdims=True) + sqk * v_row
        den = a * jnp.sum(q_row * n_row, axis=1, keepdims=True) + sqk
        hh = num / jnp.maximum(jnp.abs(den), jnp.exp(-m_t))
        hn = hh * lax.rsqrt(jnp.mean(hh * hh, axis=-1, keepdims=True) + EPS) * gh_ref[h:h + 1, :]
        hs_ref[h:h + 1, :] = (hn * jax.nn.sigmoid(o_ref[h:h + 1, :])).astype(hs_ref.dtype)
        cn_ref[h] = a * c_h + (w * k_col) * v_row
        nn_ref[h:h + 1, :] = a * n_row + w * k_row
        mn_ref[:, h:h + 1] = m_t


def mlstm_sample(q, k, v, o, gates, g_head, c0, n0, m0):
    bsz, nh, dqk, dv = c0.shape
    row3 = lambda w: pl.BlockSpec((None, nh, w), lambda b: (b, 0, 0))
    return pl.pallas_call(
        _mlstm_step_kernel,
        grid=(bsz,),
        in_specs=[row3(dqk), row3(dqk), row3(dv), row3(dv),
                  pl.BlockSpec((None, 1, 2 * nh), lambda b: (b, 0, 0)),
                  pl.BlockSpec((None, 1, nh), lambda b: (b, 0, 0)),
                  pl.BlockSpec((nh, dv), lambda b: (0, 0)),
                  pl.BlockSpec((None, nh, dqk, dv), lambda b: (b, 0, 0, 0)),
                  row3(dqk)],
        out_specs=[row3(dv),
                   pl.BlockSpec((None, nh, dqk, dv), lambda b: (b, 0, 0, 0)),
                   row3(dqk),
                   pl.BlockSpec((None, 1, nh), lambda b: (b, 0, 0))],
        out_shape=[jax.ShapeDtypeStruct((bsz, nh, dv), F32),
                   jax.ShapeDtypeStruct((bsz, nh, dqk, dv), F32),
                   jax.ShapeDtypeStruct((bsz, nh, dqk), F32),
                   jax.ShapeDtypeStruct((bsz, 1, nh), F32)],
        compiler_params=_params("parallel"),
        name="mlstm_sample",
    )(q, k, v, o, gates, m0, g_head.reshape(nh, dv), c0, n0)


def _bias_tiles_kernel(rel_ref, o_ref):
    d = pl.program_id(0)
    nh, t, _ = o_ref.shape
    dist = d * t + lax.broadcasted_iota(jnp.int32, (t, t), 0) - lax.broadcasted_iota(jnp.int32, (t, t), 1)
    bucket = _rel_bucket(dist)
    for h in range(nh):
        o_ref[h] = jnp.where(dist >= 0, _bias_lookup(bucket, rel_ref, h), NEG_BIG)


def bias_tiles(rel_bias, nb, t):
    nh = rel_bias.shape[1]
    return pl.pallas_call(
        _bias_tiles_kernel,
        grid=(nb,),
        in_specs=[pl.BlockSpec(memory_space=pltpu.SMEM)],
        out_specs=pl.BlockSpec((nh, None, t, t), lambda d: (0, d, 0, 0)),
        out_shape=jax.ShapeDtypeStruct((nh, nb, t, t), F32),
        compiler_params=_params("parallel"),
        name="bias_tiles",
    )(rel_bias)


def _kmeans_kernel(k_ref, o_ref):
    nblk = o_ref.shape[0]
    t = k_ref.shape[0] // nblk
    for j in range(nblk):
        o_ref[j:j + 1, :] = jnp.sum(k_ref[j * t:(j + 1) * t, :], axis=0, keepdims=True) * (1.0 / t)


def block_means(k, blocks_per_step=8):
    m, w = k.shape
    rows = MOBA_BLOCK * blocks_per_step
    assert m % rows == 0
    return pl.pallas_call(
        _kmeans_kernel,
        grid=(m // rows,),
        in_specs=[pl.BlockSpec((rows, w), lambda i: (i, 0))],
        out_specs=pl.BlockSpec((blocks_per_step, w), lambda i: (i, 0)),
        out_shape=jax.ShapeDtypeStruct((m // MOBA_BLOCK, w), F32),
        compiler_params=_params("parallel"),
        name="block_means",
    )(k)


def _top_k_rows(gate, n_avail, k_sel):
    nb = gate.shape[0]
    jidx = lax.broadcasted_iota(jnp.int32, gate.shape, 0).astype(F32)
    selected = jnp.zeros(gate.shape, F32)
    picks = []
    for r in range(k_sel):
        mx = jnp.max(gate, axis=0, keepdims=True)
        idx = jnp.min(jnp.where(gate == mx, jidx, float(nb)), axis=0, keepdims=True)
        hit = jidx == idx
        selected = jnp.where(jnp.logical_and(hit, r < n_avail), 1.0, selected)
        gate = jnp.where(hit, -jnp.inf, gate)
        picks.append(idx)
    return selected > 0.5, picks


def _moba_select_kernel(q_ref, k_ref, v_ref, km_ref, qa_ref, ka_ref, vb_ref, *, nb, k_sel):
    t = pl.program_id(2)
    tq = q_ref.shape[0]
    half = LANES // 2
    shift = MOBA_BLOCK.bit_length() - 1
    q2, k2 = q_ref[...], k_ref[...]
    km = km_ref[...]
    lane = lax.broadcasted_iota(jnp.int32, (tq, LANES), 1)
    rowblk = (t * tq + lax.broadcasted_iota(jnp.int32, (tq, LANES), 0)) >> shift
    km_lane = lax.broadcasted_iota(jnp.int32, km.shape, 1)
    pos = t * tq + lax.broadcasted_iota(jnp.int32, (1, tq), 1)
    n_avail = jnp.minimum(pos >> shift, nb)
    jidx = lax.broadcasted_iota(jnp.int32, (nb, tq), 0)
    vb_ref[...] = v_ref[...].astype(BF16)
    for hh in range(2):
        own = (lane >= half) if hh else (lane < half)
        off = 0 if hh else half
        kmh = jnp.where((km_lane >= half) if hh else (km_lane < half), km, 0.0)
        gate = _dot_f32(kmh, q2, _dot_nt)
        gate = jnp.where(jidx < n_avail, gate, -jnp.inf)
        selected, _ = _top_k_rows(gate, n_avail, k_sel)
        code = jnp.where(jnp.logical_or(selected, jidx == (pos >> shift)), 0.0, NEG_BIG)
        pieces = [code, jnp.zeros((LANES - nb - off, tq), F32)]
        if off:
            pieces = [jnp.zeros((off, tq), F32)] + pieces
        code_t = jnp.concatenate(pieces, axis=0).T
        qa_ref[hh] = jnp.where(own, q2 * (half ** -0.5), code_t).astype(BF16)
        onehot = jnp.where((lane - off) == rowblk, 1.0, 0.0)
        ka_ref[hh] = jnp.where(own, k2, onehot).astype(BF16)


def moba_select(q, k, v, kmeans, batch, tq=2048):
    m, w = q.shape
    s = m // batch
    nb = s // MOBA_BLOCK
    npair = w // LANES
    tq = min(tq, s)
    nt = s // tq
    assert nb <= LANES // 2 and s % tq == 0
    k_sel = min(MOBA_TOPK, nb)
    rows = lambda: pl.BlockSpec((tq, LANES), lambda b, p, t: (b * nt + t, p))
    aug = lambda: pl.BlockSpec((None, 2, tq, LANES), lambda b, p, t: (b, p, t, 0))
    return pl.pallas_call(
        functools.partial(_moba_select_kernel, nb=nb, k_sel=k_sel),
        grid=(batch, npair, nt),
        in_specs=[rows(), rows(), rows(),
                  pl.BlockSpec((nb, LANES), lambda b, p, t: (b, p))],
        out_specs=[aug(), aug(), pl.BlockSpec((None, None, tq, LANES), lambda b, p, t: (b, p, t, 0))],
        out_shape=[jax.ShapeDtypeStruct((batch, 2 * npair, s, LANES), BF16),
                   jax.ShapeDtypeStruct((batch, 2 * npair, s, LANES), BF16),
                   jax.ShapeDtypeStruct((batch, npair, s, LANES), BF16)],
        compiler_params=_params("parallel", "parallel", "parallel"),
        name="moba_select",
    )(q, k, v, kmeans)


def _moba_attn_kernel(qa_ref, ka_ref, v_ref, bias_ref, o_ref):
    i = pl.program_id(2)
    t = qa_ref.shape[1]
    half = LANES // 2
    q0, q1 = qa_ref[0], qa_ref[1]

    def body(d, carry):
        m0, l0, acc0, m1, l1, acc1 = carry
        start = pl.multiple_of((i - d) * t, t)
        vblk = v_ref[pl.ds(start, t), :]
        out = []
        for hh, (q, m, l, acc) in enumerate(((q0, m0, l0, acc0), (q1, m1, l1, acc1))):
            s = _dot_nt(q, ka_ref[hh, pl.ds(start, t), :]) + bias_ref[hh, d]
            m_new = jnp.maximum(m, jnp.max(s, axis=1, keepdims=True))
            alpha = jnp.exp(m - m_new)
            p = jnp.exp(s - m_new)
            l = alpha * l + jnp.sum(p, axis=1, keepdims=True)
            acc = alpha * acc + _dot(p.astype(BF16), vblk)
            out += [m_new, l, acc]
        return tuple(out)

    init = (jnp.full((t, 1), NEG_BIG, F32), jnp.zeros((t, 1), F32), jnp.zeros((t, LANES), F32)) * 2
    _, l0, acc0, _, l1, acc1 = lax.fori_loop(0, i + 1, body, init)
    lane = lax.broadcasted_iota(jnp.int32, (t, LANES), 1)
    o_ref[...] = jnp.where(lane < half, acc0 / l0, acc1 / l1).astype(o_ref.dtype)


def moba_attn(qa, ka, vb, bias):
    batch, nh, s, _ = qa.shape
    npair = nh // 2
    t = MOBA_BLOCK
    nb = s // t
    return pl.pallas_call(
        _moba_attn_kernel,
        grid=(npair, batch, nb),
        in_specs=[pl.BlockSpec((None, 2, t, LANES), lambda p, b, i: (b, p, i, 0)),
                  pl.BlockSpec((None, 2, s, LANES), lambda p, b, i: (b, p, 0, 0)),
                  pl.BlockSpec((None, None, s, LANES), lambda p, b, i: (b, p, 0, 0)),
                  pl.BlockSpec((2, nb, t, t), lambda p, b, i: (p, 0, 0, 0))],
        out_specs=pl.BlockSpec((t, LANES), lambda p, b, i: (b * nb + i, p)),
        out_shape=jax.ShapeDtypeStruct((batch * s, npair * LANES), BF16),
        compiler_params=_params("parallel", "parallel", "arbitrary"),
        name="moba_attn",
    )(qa, ka, vb, bias)


HEAD_GROUP = 8


def _page_means_kernel(pt_ref, p0_ref, p1_ref, o_ref):
    j = pl.program_id(1)
    rows = p0_ref.shape[0] + p1_ref.shape[0]
    tot = jnp.sum(p0_ref[...], axis=0) + jnp.sum(p1_ref[...], axis=0)
    o_ref[pl.ds(j, 1)] = (tot * (1.0 / rows))[None]


def page_means(cache, layer, page_table, nbc):
    bsz = page_table.shape[0]
    _, _, ps, nh, dh = cache.shape
    assert MOBA_BLOCK == 2 * ps
    page = lambda p: pl.BlockSpec((None, None, ps, nh, dh), lambda b, j, pt: (layer, pt[b, 2 * j + p], 0, 0, 0))
    return pl.pallas_call(
        _page_means_kernel,
        grid_spec=pltpu.PrefetchScalarGridSpec(
            num_scalar_prefetch=1,
            grid=(bsz, nbc),
            in_specs=[page(0), page(1)],
            out_specs=pl.BlockSpec((None, nbc, nh, dh), lambda b, j, pt: (b, 0, 0, 0))),
        out_shape=jax.ShapeDtypeStruct((bsz, nbc, nh, dh), F32),
        compiler_params=_params("parallel", "arbitrary"),
        name="page_means",
    )(page_table, cache, cache)


def _sample_select_kernel(q_ref, km_ref, sel_ref, *, k_sel):
    nb, nh, _ = km_ref.shape
    gate = jnp.sum(km_ref[...] * q_ref[...][None], axis=-1)
    n_avail = jnp.full((1, nh), nb, jnp.int32)
    _, picks = _top_k_rows(gate, n_avail, k_sel)
    for r, idx in enumerate(picks):
        sel_ref[r:r + 1, :] = idx.astype(jnp.int32)


def sample_select(q, kmeans, k_sel):
    bsz, nb, nh, dh = kmeans.shape
    return pl.pallas_call(
        functools.partial(_sample_select_kernel, k_sel=k_sel),
        grid=(bsz,),
        in_specs=[pl.BlockSpec((None, nh, dh), lambda b: (b, 0, 0)),
                  pl.BlockSpec((None, nb, nh, dh), lambda b: (b, 0, 0, 0))],
        out_specs=pl.BlockSpec((None, k_sel, nh), lambda b: (b, 0, 0)),
        out_shape=jax.ShapeDtypeStruct((bsz, k_sel, nh), jnp.int32),
        compiler_params=_params("parallel"),
        name="sample_select",
    )(q, kmeans)


def _sample_attn_kernel(pg_ref, sel_ref, rel_ref, q_ref, kn_ref, vn_ref, *refs, nh, k_sel, past):
    npg = 2 * k_sel
    k_refs, v_refs, o_ref = refs[:npg], refs[npg:2 * npg], refs[2 * npg]
    b, h = pl.program_id(0), pl.program_id(1)
    ps, hg, dh = k_refs[0].shape
    assert hg & (hg - 1) == 0
    n = ps * hg
    hl = h % hg
    rowsel = lax.broadcasted_iota(jnp.int32, (hg, dh), 0) == hl
    qm = jnp.where(rowsel, q_ref[...] * (dh ** -0.5), 0.0)
    q_row = jnp.sum(qm, axis=0, keepdims=True)
    kn = jnp.sum(jnp.where(rowsel, kn_ref[...], 0.0), axis=0, keepdims=True)
    vn = jnp.sum(jnp.where(rowsel, vn_ref[...], 0.0), axis=0, keepdims=True)
    lane = lax.broadcasted_iota(jnp.int32, (1, n), 1)
    mine = (lane & (hg - 1)) == hl
    key_row = lane >> (hg.bit_length() - 1)
    scores = []
    for r in range(k_sel):
        blk = sel_ref[(b * nh + h) * k_sel + r]
        for p in range(2):
            s = jnp.sum(_dot_f32(qm, k_refs[2 * r + p][...].reshape(n, dh), _dot_nt), axis=0, keepdims=True)
            dist = past - (blk * MOBA_BLOCK + p * ps + key_row)
            scores.append(jnp.where(mine, s + _bias_lookup(_rel_bucket(dist), rel_ref, h), NEG_BIG))
    s_self = jnp.sum(q_row * kn, axis=1, keepdims=True) + rel_ref[0, h]
    mx = s_self
    for s in scores:
        mx = jnp.maximum(mx, jnp.max(s, axis=1, keepdims=True))
    p_self = jnp.exp(s_self - mx)
    den = p_self
    acc = p_self * vn
    for s, v_ref in zip(scores, v_refs):
        p = jnp.exp(s - mx)
        den = den + jnp.sum(p, axis=1, keepdims=True)
        acc = acc + _dot_f32(jnp.broadcast_to(p, (8, n)), v_ref[...].reshape(n, dh))[0:1, :]
    o_ref[...] = acc / den


def sample_attn(pages, sel_flat, rel_bias, q, k_new, v_new, cache_k, cache_v, layer, k_sel, past):
    bsz, nh, dh = q.shape
    ps = cache_k.shape[2]
    hg = HEAD_GROUP
    npg = 2 * k_sel

    def page(i):
        return pl.BlockSpec((None, None, ps, hg, dh),
                            lambda b, h, pg, sel: (layer, pg[(b * nh + h) * npg + i], 0, h // hg, 0))

    new = lambda: pl.BlockSpec((None, hg, dh), lambda b, h, pg, sel: (b, h // hg, 0))
    return pl.pallas_call(
        functools.partial(_sample_attn_kernel, nh=nh, k_sel=k_sel, past=past),
        grid_spec=pltpu.PrefetchScalarGridSpec(
            num_scalar_prefetch=2,
            grid=(bsz, nh),
            in_specs=[pl.BlockSpec(memory_space=pltpu.SMEM), new(), new(), new()]
                     + [page(i) for i in range(npg)] * 2,
            out_specs=pl.BlockSpec((None, None, 1, dh), lambda b, h, pg, sel: (b, h, 0, 0))),
        out_shape=jax.ShapeDtypeStruct((bsz, nh, 1, dh), F32),
        compiler_params=_params("parallel", "arbitrary"),
        name="sample_attn",
    )(pages, sel_flat, rel_bias, q, k_new, v_new, *([cache_k] * npg), *([cache_v] * npg))


def _mlstm_layer(hp, hs, batch, st_c, st_n, st_m, g_mix, w_in, b_gates, g_head, w_out):
    nh = ML_HEADS
    d = hp.shape[1]
    vw = w_out.shape[0]
    dv = vw // nh
    dqk = dv // 2
    qkw = nh * dqk
    w_main = jnp.concatenate([w_in[:, :qkw], w_in[:, 2 * qkw:2 * qkw + 2 * vw]], axis=1).astype(BF16)
    wkt = w_in[:, qkw:2 * qkw].T.astype(BF16)
    wgt = w_in[:, 2 * qkw + 2 * vw:].T
    w_out_b = w_out.astype(BF16)

    q, v, o, kt, gt = ml_proj(hp, g_mix, w_main, wkt, wgt, b_gates)
    hsp, c_p, n_p, m_p = mlstm_prompt(q, v, o, kt, gt, g_head, batch)
    hp = out_proj(hsp, w_out_b, hp)

    bs = hs.shape[0]
    q, v, o, kt, gt = ml_proj(hs, g_mix, w_main, wkt, wgt, b_gates)
    hss, c_s, n_s, m_s = mlstm_sample(q.reshape(bs, nh, dqk), kt.T.reshape(bs, nh, dqk), v.reshape(bs, nh, dv),
                                      o.reshape(bs, nh, dv), gt.T.reshape(bs, 1, 2 * nh), g_head,
                                      st_c, st_n, st_m.reshape(bs, 1, nh))
    hs = out_proj(hss.reshape(bs, vw), w_out_b, hs)
    return hp, hs, (c_p, n_p, m_p), (c_s, n_s, m_s.reshape(bs, nh))


def _moba_layer(hp, hs, batch, cache_k, cache_v, layer, page_table, bias, rel_bias, g_mix, w_qkv, w_out, nh):
    d = hp.shape[1]
    w = w_qkv.shape[1] // 3
    dh = w // nh
    assert dh == LANES // 2
    w_qkv_b = w_qkv.astype(BF16)
    w_out_b = w_out.astype(BF16)

    q, k, v = norm_proj(hp, g_mix, w_qkv_b, (w, w, w))
    kmeans = block_means(k)
    qa, ka, vb = moba_select(q, k, v, kmeans, batch)
    o = moba_attn(qa, ka, vb, bias)
    hp = out_proj(o, w_out_b, hp)

    bs, n_pages = page_table.shape
    ps = cache_k.shape[2]
    past = n_pages * ps
    nbc = past // MOBA_BLOCK
    assert past == nbc * MOBA_BLOCK and nbc >= 1, "a partially filled tail block is not supported"
    k_sel = min(MOBA_TOPK, nbc)
    (qkv_s,) = norm_proj(hs, g_mix, w_qkv_b, (3 * w,))
    q_s, k_s, v_s = (qkv_s[:, i * w:(i + 1) * w].reshape(bs, nh, dh) for i in range(3))
    km_s = page_means(cache_k, layer, page_table, nbc)
    sel = sample_select(q_s, km_s, k_sel)
    sel_bhk = jnp.transpose(sel, (0, 2, 1))
    logical = sel_bhk[..., None] * 2 + jnp.arange(2, dtype=jnp.int32)
    pages = jnp.take_along_axis(page_table, logical.reshape(bs, -1), axis=1)
    o_s = sample_attn(pages.reshape(-1), sel_bhk.reshape(-1), rel_bias, q_s, k_s, v_s,
                      cache_k, cache_v, layer, k_sel, past)
    hs = out_proj(o_s.reshape(bs, w), w_out_b, hs)
    return hp, hs, (k, v), (k_s, v_s)


def kernel(x_prompt, x_sample, cache_k, cache_v, state_C, state_n, state_m, page_table, norm_mix, norm_ffn,
           norm_final, rel_bias, w_ml_in, b_ml_gates, g_ml_head, w_ml_out, w_attn_qkv, w_attn_out, w_ffn_in,
           w_ffn_out):
    batch, seq, d = x_prompt.shape
    bs, dec_seq, _ = x_sample.shape
    assert dec_seq == 1 and seq % MOBA_BLOCK == 0
    depth = norm_mix.shape[0]
    nh_at = cache_k.shape[3]
    dh = cache_k.shape[4]
    hp = x_prompt.reshape(batch * seq, d)
    hs = x_sample.reshape(bs, d)
    w_ffn_in_b = w_ffn_in.astype(BF16)
    w_ffn_out_b = w_ffn_out.astype(BF16)
    bias = bias_tiles(rel_bias, seq // MOBA_BLOCK, MOBA_BLOCK)

    kv_p, kv_s, st_p, st_s = [], [], [], []
    for layer in range(depth):
        j = layer // 2
        if layer % 2 == 0:
            hp, hs, sp, ss = _mlstm_layer(hp, hs, batch, state_C[j], state_n[j], state_m[j], norm_mix[layer],
                                          w_ml_in[j], b_ml_gates[j], g_ml_head[j], w_ml_out[j])
            st_p.append(sp)
            st_s.append(ss)
        else:
            hp, hs, kp, ks = _moba_layer(hp, hs, batch, cache_k, cache_v, j, page_table, bias, rel_bias,
                                         norm_mix[layer], w_attn_qkv[j], w_attn_out[j], nh_at)
            kv_p.append(kp)
            kv_s.append(ks)
        hp = ffn(hp, norm_ffn[layer], w_ffn_in_b[layer], w_ffn_out_b[layer])
        hs = ffn(hs, norm_ffn[layer], w_ffn_in_b[layer], w_ffn_out_b[layer])

    y_prompt = rmsnorm(hp, norm_final).reshape(batch, seq, d)
    y_sample = rmsnorm(hs, norm_final).reshape(bs, dec_seq, d)
    stack = lambda items, i, shape: jnp.stack([it[i] for it in items]).reshape((len(items),) + shape)
    return (y_prompt, y_sample,
            stack(kv_p, 0, (batch, seq, nh_at, dh)), stack(kv_p, 1, (batch, seq, nh_at, dh)),
            stack(kv_s, 0, (bs, dec_seq, nh_at, dh)), stack(kv_s, 1, (bs, dec_seq, nh_at, dh)),
            jnp.stack([s[0] for s in st_p]), jnp.stack([s[1] for s in st_p]), jnp.stack([s[2] for s in st_p]),
            jnp.stack([s[0] for s in st_s]), jnp.stack([s[1] for s in st_s]), jnp.stack([s[2] for s in st_s]))
```

```python
import functools
import math

import jax
import jax.numpy as jnp
from jax import lax
from jax.experimental import pallas as pl
from jax.experimental.pallas import tpu as pltpu

F32 = jnp.float32
BF16 = jnp.bfloat16

EPS = 1e-6
ML_HEADS = 8
ML_CHUNK = 256
MOBA_BLOCK = 256
MOBA_TOPK = 3
KV_GROUP = 4
NUM_BUCKETS = 32
MAX_DISTANCE = 4096
NEG_BIG = -1e30
LANES = 128
VMEM_LIMIT = 56 * 1024 * 1024


def _dot(a, b):
    return jnp.dot(a, b, preferred_element_type=F32)


def _dot_nt(a, b):
    return lax.dot_general(a, b, (((1,), (1,)), ((), ())), preferred_element_type=F32)


def _split2(x):
    hi = x.astype(BF16)
    lo = (x - hi.astype(F32)).astype(BF16)
    return hi, lo


def _split3(x):
    hi = x.astype(BF16)
    r = x - hi.astype(F32)
    mid = r.astype(BF16)
    lo = (r - mid.astype(F32)).astype(BF16)
    return hi, mid, lo


def _dot_f32(a, b, dot=_dot):
    ah, al = _split2(a)
    bh, bl = _split2(b)
    return dot(ah, bh) + dot(ah, bl) + dot(al, bh)


def _norm_rows(x, g):
    return x * lax.rsqrt(jnp.mean(x * x, axis=-1, keepdims=True) + EPS) * g


def _log_sigmoid(x):
    return jnp.minimum(x, 0.0) - jnp.log1p(jnp.exp(-jnp.abs(x)))


def _rel_bucket(dist):
    n = jnp.maximum(dist, 0)
    max_exact = NUM_BUCKETS // 2
    nf = jnp.maximum(n, 1).astype(F32)
    large = max_exact + (jnp.log(nf / max_exact) / math.log(MAX_DISTANCE / max_exact)
                         * (NUM_BUCKETS - max_exact)).astype(jnp.int32)
    return jnp.where(n < max_exact, n, jnp.minimum(large, NUM_BUCKETS - 1))


def _bias_lookup(bucket, rel_ref, h):
    acc = jnp.full(bucket.shape, rel_ref[0, h], F32)
    for k in range(1, NUM_BUCKETS):
        acc = jnp.where(bucket == k, rel_ref[k, h], acc)
    return acc


def _params(*sem):
    return pltpu.CompilerParams(dimension_semantics=sem, vmem_limit_bytes=VMEM_LIMIT)


def _row_tile(m, target):
    t = min(m, target)
    assert m % t == 0, (m, t)
    return t


def _proj_kernel(x_ref, g_ref, w_ref, *o_refs):
    xb = _norm_rows(x_ref[...], g_ref[...]).astype(BF16)
    c0 = 0
    for o_ref in o_refs:
        n = o_ref.shape[1]
        for c in range(0, n, 512):
            w = min(512, n - c)
            o_ref[:, c:c + w] = _dot(xb, w_ref[:, c0 + c:c0 + c + w])
        c0 += n


def norm_proj(x, g, w, widths, tm=512):
    m, d = x.shape
    tm = _row_tile(m, tm)
    n = w.shape[1]
    assert sum(widths) == n
    return pl.pallas_call(
        _proj_kernel,
        grid=(m // tm,),
        in_specs=[pl.BlockSpec((tm, d), lambda i: (i, 0)),
                  pl.BlockSpec((1, d), lambda i: (0, 0)),
                  pl.BlockSpec((d, n), lambda i: (0, 0))],
        out_specs=[pl.BlockSpec((tm, wd), lambda i: (i, 0)) for wd in widths],
        out_shape=[jax.ShapeDtypeStruct((m, wd), F32) for wd in widths],
        compiler_params=_params("parallel"),
        name="norm_proj",
    )(x, g.reshape(1, d), w)


def _ml_proj_kernel(x_ref, g_ref, w_ref, wkt_ref, wgt_ref, bg_ref, q_ref, v_ref, o_ref, kt_ref, gt_ref):
    xn = _norm_rows(x_ref[...], g_ref[...])
    xb = xn.astype(BF16)
    xlo = (xn - xb.astype(F32)).astype(BF16)
    c0 = 0
    for out in (q_ref, v_ref, o_ref):
        n = out.shape[1]
        for c in range(0, n, 512):
            out[:, c:c + 512] = _dot(xb, w_ref[:, c0 + c:c0 + c + 512])
        c0 += n
    kt_ref[...] = _dot_nt(wkt_ref[...], xb)
    gh, gl = _split2(wgt_ref[...])
    gt_ref[...] = _dot_nt(gh, xb) + _dot_nt(gh, xlo) + _dot_nt(gl, xb) + bg_ref[...]


def ml_proj(x, g, w_main, wkt, wgt, bg, tm=512):
    m, d = x.shape
    tm = _row_tile(m, tm)
    qk = wkt.shape[0]
    vw = (w_main.shape[1] - qk) // 2
    ng = wgt.shape[0]
    return pl.pallas_call(
        _ml_proj_kernel,
        grid=(m // tm,),
        in_specs=[pl.BlockSpec((tm, d), lambda i: (i, 0)),
                  pl.BlockSpec((1, d), lambda i: (0, 0)),
                  pl.BlockSpec(w_main.shape, lambda i: (0, 0)),
                  pl.BlockSpec(wkt.shape, lambda i: (0, 0)),
                  pl.BlockSpec(wgt.shape, lambda i: (0, 0)),
                  pl.BlockSpec((ng, 1), lambda i: (0, 0))],
        out_specs=[pl.BlockSpec((tm, qk), lambda i: (i, 0)),
                   pl.BlockSpec((tm, vw), lambda i: (i, 0)),
                   pl.BlockSpec((tm, vw), lambda i: (i, 0)),
                   pl.BlockSpec((qk, tm), lambda i: (0, i)),
                   pl.BlockSpec((ng, tm), lambda i: (0, i))],
        out_shape=[jax.ShapeDtypeStruct((m, qk), F32),
                   jax.ShapeDtypeStruct((m, vw), F32),
                   jax.ShapeDtypeStruct((m, vw), F32),
                   jax.ShapeDtypeStruct((qk, m), F32),
                   jax.ShapeDtypeStruct((ng, m), F32)],
        compiler_params=_params("parallel"),
        name="ml_proj",
    )(x, g.reshape(1, d), w_main, wkt, wgt, bg.reshape(ng, 1))


def _out_proj_kernel(a_ref, w_ref, r_ref, o_ref):
    o_ref[...] = r_ref[...] + _dot(a_ref[...].astype(BF16), w_ref[...])


def out_proj(a, w, res, tm=1024):
    m, k = a.shape
    d = w.shape[1]
    tm = _row_tile(m, tm)
    return pl.pallas_call(
        _out_proj_kernel,
        grid=(m // tm,),
        in_specs=[pl.BlockSpec((tm, k), lambda i: (i, 0)),
                  pl.BlockSpec((k, d), lambda i: (0, 0)),
                  pl.BlockSpec((tm, d), lambda i: (i, 0))],
        out_specs=pl.BlockSpec((tm, d), lambda i: (i, 0)),
        out_shape=jax.ShapeDtypeStruct((m, d), F32),
        compiler_params=_params("parallel"),
        name="out_proj",
    )(a, w, res)


def _ffn_kernel(x_ref, g_ref, wg_ref, wu_ref, wo_ref, o_ref, xn_s, acc_s):
    f = pl.program_id(1)

    @pl.when(f == 0)
    def _():
        xn_s[...] = _norm_rows(x_ref[...], g_ref[...]).astype(BF16)
        acc_s[...] = jnp.zeros_like(acc_s)

    xb = xn_s[...]
    gp = _dot(xb, wg_ref[...])
    up = _dot(xb, wu_ref[...])
    act = (gp * jax.nn.sigmoid(gp) * up).astype(BF16)
    acc_s[...] += _dot(act, wo_ref[...])

    @pl.when(f == pl.num_programs(1) - 1)
    def _():
        o_ref[...] = x_ref[...] + acc_s[...]


def ffn(x, g, w_in, w_out, tm=1024, tf=256):
    m, d = x.shape
    dff = w_out.shape[0]
    tm = _row_tile(m, tm)
    nf = dff // tf
    assert dff % tf == 0
    return pl.pallas_call(
        _ffn_kernel,
        grid=(m // tm, nf),
        in_specs=[pl.BlockSpec((tm, d), lambda i, f: (i, 0)),
                  pl.BlockSpec((1, d), lambda i, f: (0, 0)),
                  pl.BlockSpec((d, tf), lambda i, f: (0, f)),
                  pl.BlockSpec((d, tf), lambda i, f: (0, nf + f)),
                  pl.BlockSpec((tf, d), lambda i, f: (f, 0))],
        out_specs=pl.BlockSpec((tm, d), lambda i, f: (i, 0)),
        out_shape=jax.ShapeDtypeStruct((m, d), F32),
        scratch_shapes=[pltpu.VMEM((tm, d), BF16), pltpu.VMEM((tm, d), F32)],
        compiler_params=_params("parallel", "arbitrary"),
        name="ffn",
    )(x, g.reshape(1, d), w_in, w_in, w_out)


def _rmsnorm_kernel(x_ref, g_ref, o_ref):
    o_ref[...] = _norm_rows(x_ref[...], g_ref[...])


def rmsnorm(x, g, tm=1024):
    m, d = x.shape
    tm = _row_tile(m, tm)
    return pl.pallas_call(
        _rmsnorm_kernel,
        grid=(m // tm,),
        in_specs=[pl.BlockSpec((tm, d), lambda i: (i, 0)), pl.BlockSpec((1, d), lambda i: (0, 0))],
        out_specs=pl.BlockSpec((tm, d), lambda i: (i, 0)),
        out_shape=jax.ShapeDtypeStruct((m, d), F32),
        compiler_params=_params("parallel"),
        name="rmsnorm",
    )(x, g.reshape(1, d))


def _mlstm_chunk_kernel(q_ref, v_ref, o_ref, kt_ref, gt_ref, gh_ref, hs_ref, caug_ref, m_ref, caug_s, m_s):
    c = pl.program_id(1)
    L = q_ref.shape[0]
    nh = caug_s.shape[0]
    dqk = caug_s.shape[1]
    dv = caug_s.shape[2] // 2

    @pl.when(c == 0)
    def _():
        caug_s[...] = jnp.zeros_like(caug_s)
        m_s[...] = jnp.zeros_like(m_s)

    row = lax.broadcasted_iota(jnp.int32, (L, L), 0)
    col = lax.broadcasted_iota(jnp.int32, (L, L), 1)
    causal = col <= row
    lower = jnp.where(causal, 1.0, 0.0).astype(BF16)
    upper = jnp.where(row <= col, 1.0, 0.0).astype(BF16)
    gt = gt_ref[...]
    ig = gt[0:nh, :]
    logf = _log_sigmoid(gt[nh:2 * nh, :])
    parts = _split3(logf)
    f_row = _dot(parts[0], upper) + _dot(parts[1], upper) + _dot(parts[2], upper)
    f_col = _dot_nt(lower, parts[0]) + _dot_nt(lower, parts[1]) + _dot_nt(lower, parts[2])
    a_rows = ig - f_row
    ones_col = jnp.where(lax.broadcasted_iota(jnp.int32, (L, dv), 1) == 0, 1.0, 0.0).astype(BF16)

    for h in range(nh):
        m_prev = m_s[h:h + 1, 0:1]
        a_row = a_rows[h:h + 1, :]
        amat = jnp.where(causal, a_row, -jnp.inf)
        g = jnp.maximum(jnp.max(amat, axis=1, keepdims=True), m_prev)
        w = jnp.exp(amat - g)
        qh = q_ref[:, h * dqk:(h + 1) * dqk].astype(BF16)
        kth = kt_ref[h * dqk:(h + 1) * dqk, :] * (dqk ** -0.5)
        p = (_dot(qh, kth.astype(BF16)) * w).astype(BF16)
        vaug = jnp.concatenate([v_ref[:, h * dv:(h + 1) * dv].astype(BF16), ones_col], axis=1)
        caug = caug_s[h]
        a_int = jnp.exp(m_prev - g)
        tot = a_int * _dot(qh, caug.astype(BF16)) + _dot(p, vaug)
        m_t = f_col[:, h:h + 1] + g
        hh = tot[:, :dv] / jnp.maximum(jnp.abs(tot[:, dv:dv + 1]), jnp.exp(-m_t))
        hn = hh * lax.rsqrt(jnp.mean(hh * hh, axis=-1, keepdims=True) + EPS) * gh_ref[:, h * dv:(h + 1) * dv]
        hs_ref[:, h * dv:(h + 1) * dv] = (hn * jax.nn.sigmoid(o_ref[:, h * dv:(h + 1) * dv])).astype(hs_ref.dtype)
        g_last = g[L - 1:L, :]
        w_last = jnp.exp(a_row - g_last)
        caug_s[h] = jnp.exp(m_prev - g_last) * caug + _dot((kth * w_last).astype(BF16), vaug)
        m_s[h:h + 1, :] = jnp.broadcast_to(f_row[h:h + 1, L - 1:L] + g_last, (1, m_s.shape[1]))

    @pl.when(c == pl.num_programs(1) - 1)
    def _():
        caug_ref[...] = caug_s[...]
        m_ref[...] = m_s[...]


def mlstm_prompt(q, v, o, kt, gt, g_head, batch):
    m, qkw = q.shape
    vw = v.shape[1]
    nh = gt.shape[0] // 2
    dqk, dv = qkw // nh, vw // nh
    s = m // batch
    L = math.gcd(s, ML_CHUNK)
    nc = s // L
    hs, caug, mm = pl.pallas_call(
        _mlstm_chunk_kernel,
        grid=(batch, nc),
        in_specs=[pl.BlockSpec((L, qkw), lambda b, c: (b * nc + c, 0)),
                  pl.BlockSpec((L, vw), lambda b, c: (b * nc + c, 0)),
                  pl.BlockSpec((L, vw), lambda b, c: (b * nc + c, 0)),
                  pl.BlockSpec((qkw, L), lambda b, c: (0, b * nc + c)),
                  pl.BlockSpec((2 * nh, L), lambda b, c: (0, b * nc + c)),
                  pl.BlockSpec((1, vw), lambda b, c: (0, 0))],
        out_specs=[pl.BlockSpec((L, vw), lambda b, c: (b * nc + c, 0)),
                   pl.BlockSpec((None, nh, dqk, 2 * dv), lambda b, c: (b, 0, 0, 0)),
                   pl.BlockSpec((None, nh, LANES), lambda b, c: (b, 0, 0))],
        out_shape=[jax.ShapeDtypeStruct((m, vw), BF16),
                   jax.ShapeDtypeStruct((batch, nh, dqk, 2 * dv), F32),
                   jax.ShapeDtypeStruct((batch, nh, LANES), F32)],
        scratch_shapes=[pltpu.VMEM((nh, dqk, 2 * dv), F32), pltpu.VMEM((nh, LANES), F32)],
        compiler_params=_params("parallel", "arbitrary"),
        name="mlstm_prompt",
    )(q, v, o, kt, gt, g_head.reshape(1, vw))
    return hs, caug[..., :dv], caug[..., dv], mm[..., 0]


def _mlstm_step_kernel(q_ref, k_ref, v_ref, o_ref, gt_ref, m_ref, gh_ref, c_ref, n_ref,
                       hs_ref, cn_ref, nn_ref, mn_ref):
    nh, dqk, dv = c_ref.shape
    scale = dqk ** -0.5
    q = q_ref[...]
    k = k_ref[...] * scale
    eye = jnp.where(lax.broadcasted_iota(jnp.int32, (dqk, dqk), 0)
                    == lax.broadcasted_iota(jnp.int32, (dqk, dqk), 1), 1.0, 0.0).astype(BF16)
    q3, k3 = _split3(q), _split3(k)
    q_cols = _dot_nt(eye, q3[0]) + _dot_nt(eye, q3[1]) + _dot_nt(eye, q3[2])
    k_cols = _dot_nt(eye, k3[0]) + _dot_nt(eye, k3[1]) + _dot_nt(eye, k3[2])
    gt = gt_ref[...]
    for h in range(nh):
        ig = gt[:, h:h + 1]
        logf = _log_sigmoid(gt[:, nh + h:nh + h + 1])
        m_inter = m_ref[:, h:h + 1] + logf
        m_t = jnp.maximum(m_inter, ig)
        w = jnp.exp(ig - m_t)
        a = jnp.exp(m_inter - m_t)
        q_row, k_row = q[h:h + 1, :], k[h:h + 1, :]
        q_col, k_col = q_cols[:, h:h + 1], k_cols[:, h:h + 1]
        v_row = v_ref[h:h + 1, :]
        c_h = c_ref[h]
        n_row = n_ref[h:h + 1, :]
        sqk = jnp.sum(q_row * k_row, axis=1, keepdims=True) * w
        num = a * jnp.sum(q_col * c_h, axis=0, keepdims=True) + sqk * v_row
        den = a * jnp.sum(q_row * n_row, axis=1, keepdims=True) + sqk
        hh = num / jnp.maximum(jnp.abs(den), jnp.exp(-m_t))
        hn = hh * lax.rsqrt(jnp.mean(hh * hh, axis=-1, keepdims=True) + EPS) * gh_ref[h:h + 1, :]
        hs_ref[h:h + 1, :] = (hn * jax.nn.sigmoid(o_ref[h:h + 1, :])).astype(hs_ref.dtype)
        cn_ref[h] = a * c_h + (w * k_col) * v_row
        nn_ref[h:h + 1, :] = a * n_row + w * k_row
        mn_ref[:, h:h + 1] = m_t


def mlstm_sample(q, k, v, o, gates, g_head, c0, n0, m0):
    bsz, nh, dqk, dv = c0.shape
    row3 = lambda w: pl.BlockSpec((None, nh, w), lambda b: (b, 0, 0))
    return pl.pallas_call(
        _mlstm_step_kernel,
        grid=(bsz,),
        in_specs=[row3(dqk), row3(dqk), row3(dv), row3(dv),
                  pl.BlockSpec((None, 1, 2 * nh), lambda b: (b, 0, 0)),
                  pl.BlockSpec((None, 1, nh), lambda b: (b, 0, 0)),
                  pl.BlockSpec((nh, dv), lambda b: (0, 0)),
                  pl.BlockSpec((None, nh, dqk, dv), lambda b: (b, 0, 0, 0)),
                  row3(dqk)],
        out_specs=[row3(dv),
                   pl.BlockSpec((None, nh, dqk, dv), lambda b: (b, 0, 0, 0)),
                   row3(dqk),
                   pl.BlockSpec((None, 1, nh), lambda b: (b, 0, 0))],
        out_shape=[jax.ShapeDtypeStruct((bsz, nh, dv), F32),
                   jax.ShapeDtypeStruct((bsz, nh, dqk, dv), F32),
                   jax.ShapeDtypeStruct((bsz, nh, dqk), F32),
                   jax.ShapeDtypeStruct((bsz, 1, nh), F32)],
        compiler_params=_params("parallel"),
        name="mlstm_sample",
    )(q, k, v, o, gates, m0, g_head.reshape(nh, dv), c0, n0)


def _bias_tiles_kernel(rel_ref, o_ref, *, d_min):
    d = pl.program_id(0) + d_min
    nh, t, _ = o_ref.shape
    dist = d * t + lax.broadcasted_iota(jnp.int32, (t, t), 0) - lax.broadcasted_iota(jnp.int32, (t, t), 1)
    bucket = _rel_bucket(dist)
    for h in range(nh):
        o_ref[h] = jnp.where(dist >= 0, _bias_lookup(bucket, rel_ref, h), NEG_BIG)


def bias_tiles(rel_bias, d_min, d_max, t):
    nh = rel_bias.shape[1]
    nd = d_max - d_min + 1
    return pl.pallas_call(
        functools.partial(_bias_tiles_kernel, d_min=d_min),
        grid=(nd,),
        in_specs=[pl.BlockSpec(memory_space=pltpu.SMEM)],
        out_specs=pl.BlockSpec((nh, None, t, t), lambda d: (0, d, 0, 0)),
        out_shape=jax.ShapeDtypeStruct((nh, nd, t, t), F32),
        compiler_params=_params("parallel"),
        name="bias_tiles",
    )(rel_bias)


def _kmeans_kernel(k_ref, o_ref):
    nblk = o_ref.shape[0]
    t = k_ref.shape[0] // nblk
    for j in range(nblk):
        o_ref[j:j + 1, :] = jnp.sum(k_ref[j * t:(j + 1) * t, :], axis=0, keepdims=True) * (1.0 / t)


def block_means(k, blocks_per_step=8):
    m, w = k.shape
    rows = MOBA_BLOCK * blocks_per_step
    assert m % rows == 0
    return pl.pallas_call(
        _kmeans_kernel,
        grid=(m // rows,),
        in_specs=[pl.BlockSpec((rows, w), lambda i: (i, 0))],
        out_specs=pl.BlockSpec((blocks_per_step, w), lambda i: (i, 0)),
        out_shape=jax.ShapeDtypeStruct((m // MOBA_BLOCK, w), F32),
        compiler_params=_params("parallel"),
        name="block_means",
    )(k)


def _top_k_rows(gate, n_avail, k_sel):
    nb = gate.shape[0]
    jidx = lax.broadcasted_iota(jnp.int32, gate.shape, 0).astype(F32)
    selected = jnp.zeros(gate.shape, F32)
    picks = []
    for r in range(k_sel):
        mx = jnp.max(gate, axis=0, keepdims=True)
        idx = jnp.min(jnp.where(gate == mx, jidx, float(nb)), axis=0, keepdims=True)
        hit = jidx == idx
        selected = jnp.where(jnp.logical_and(hit, r < n_avail), 1.0, selected)
        gate = jnp.where(hit, -jnp.inf, gate)
        picks.append(idx)
    return selected > 0.5, picks


def _moba_select_kernel(q_ref, k_ref, v_ref, km_ref, kbuf_ref, vbuf_ref, qa_ref, ka_ref, vb_ref, kt_ref, vt_ref,
                        *, nb, k_sel):
    del kbuf_ref, vbuf_ref
    t = pl.program_id(2)
    tq = q_ref.shape[0]
    half = LANES // 2
    shift = MOBA_BLOCK.bit_length() - 1
    q2, k2 = q_ref[...], k_ref[...]
    kt_ref[...] = k2.T
    vt_ref[...] = v_ref[...].T
    km = km_ref[...]
    lane = lax.broadcasted_iota(jnp.int32, (tq, LANES), 1)
    rowblk = (t * tq + lax.broadcasted_iota(jnp.int32, (tq, LANES), 0)) >> shift
    km_lane = lax.broadcasted_iota(jnp.int32, km.shape, 1)
    pos = t * tq + lax.broadcasted_iota(jnp.int32, (1, tq), 1)
    n_avail = jnp.minimum(pos >> shift, nb)
    jidx = lax.broadcasted_iota(jnp.int32, (nb, tq), 0)
    vb_ref[...] = v_ref[...].astype(BF16)
    for hh in range(2):
        own = (lane >= half) if hh else (lane < half)
        off = 0 if hh else half
        kmh = jnp.where((km_lane >= half) if hh else (km_lane < half), km, 0.0)
        gate = _dot_f32(kmh, q2, _dot_nt)
        gate = jnp.where(jidx < n_avail, gate, -jnp.inf)
        selected, _ = _top_k_rows(gate, n_avail, k_sel)
        code = jnp.where(jnp.logical_or(selected, jidx == (pos >> shift)), 0.0, NEG_BIG)
        pieces = [code, jnp.zeros((LANES - nb - off, tq), F32)]
        if off:
            pieces = [jnp.zeros((off, tq), F32)] + pieces
        code_t = jnp.concatenate(pieces, axis=0).T
        qa_ref[hh] = jnp.where(own, q2 * (half ** -0.5), code_t).astype(BF16)
        onehot = jnp.where((lane - off) == rowblk, 1.0, 0.0)
        ka_ref[hh] = jnp.where(own, k2, onehot).astype(BF16)


def moba_select(q, k, v, kmeans, batch, slot, kbuf, vbuf, tq=2048):
    m, w = q.shape
    s = m // batch
    nb = s // MOBA_BLOCK
    npair = w // LANES
    tq = min(tq, s)
    nt = s // tq
    assert nb <= LANES // 2 and s % tq == 0 and kbuf.shape[1:] == (batch, w, s)
    k_sel = min(MOBA_TOPK, nb)
    rows = lambda: pl.BlockSpec((tq, LANES), lambda b, p, t: (b * nt + t, p))
    aug = lambda: pl.BlockSpec((None, 2, tq, LANES), lambda b, p, t: (b, p, t, 0))
    feat = lambda: pl.BlockSpec((None, None, LANES, tq), lambda b, p, t: (slot, b, p, t))
    return pl.pallas_call(
        functools.partial(_moba_select_kernel, nb=nb, k_sel=k_sel),
        grid=(batch, npair, nt),
        in_specs=[rows(), rows(), rows(),
                  pl.BlockSpec((nb, LANES), lambda b, p, t: (b, p)),
                  pl.BlockSpec(memory_space=pl.ANY), pl.BlockSpec(memory_space=pl.ANY)],
        out_specs=[aug(), aug(), pl.BlockSpec((None, None, tq, LANES), lambda b, p, t: (b, p, t, 0)),
                   feat(), feat()],
        out_shape=[jax.ShapeDtypeStruct((batch, 2 * npair, s, LANES), BF16),
                   jax.ShapeDtypeStruct((batch, 2 * npair, s, LANES), BF16),
                   jax.ShapeDtypeStruct((batch, npair, s, LANES), BF16),
                   jax.ShapeDtypeStruct(kbuf.shape, F32),
                   jax.ShapeDtypeStruct(vbuf.shape, F32)],
        input_output_aliases={4: 3, 5: 4},
        compiler_params=_params("parallel", "parallel", "parallel"),
        name="moba_select",
    )(q, k, v, kmeans, kbuf, vbuf)


def _moba_attn_kernel(qa_ref, ka_ref, v_ref, bias_ref, o_ref, s_scr, *, kv_group):
    i = pl.program_id(2)
    t = qa_ref.shape[1]
    gw = kv_group * t
    half = LANES // 2
    n_groups = i // kv_group + 1
    qs = (qa_ref[0], qa_ref[1])

    def scores(g, rmax):
        start = pl.multiple_of(g * gw, gw)
        rmax = list(rmax)
        for hh in range(2):
            s = _dot_nt(qs[hh], ka_ref[hh, pl.ds(start, gw), :])
            for jj in range(kv_group):
                j = g * kv_group + jj
                sj = s[:, jj * t:(jj + 1) * t] + bias_ref[hh, i - j + kv_group - 1]
                s_scr[hh, j] = sj
                for c in range(0, t, LANES):
                    rmax[hh] = jnp.maximum(rmax[hh], sj[:, c:c + LANES])
        return tuple(rmax)

    rmax = lax.fori_loop(0, n_groups, scores, (jnp.full((t, LANES), NEG_BIG, F32),) * 2)
    ms = [jnp.broadcast_to(jnp.max(r, axis=1, keepdims=True), (t, LANES)) for r in rmax]

    def values(g, carry):
        start = pl.multiple_of(g * gw, gw)
        vgrp = v_ref[pl.ds(start, gw), :]
        out = []
        for hh in range(2):
            rsum, acc = carry[2 * hh], carry[2 * hh + 1]
            ps = []
            for jj in range(kv_group):
                sj = s_scr[hh, g * kv_group + jj]
                for c in range(0, t, LANES):
                    pc = jnp.exp(sj[:, c:c + LANES] - ms[hh])
                    rsum = rsum + pc
                    ps.append(pc.astype(BF16))
            acc = acc + _dot(jnp.concatenate(ps, axis=1), vgrp)
            out += [rsum, acc]
        return tuple(out)

    zero = jnp.zeros((t, LANES), F32)
    rsum0, acc0, rsum1, acc1 = lax.fori_loop(0, n_groups, values, (zero,) * 4)
    lane = lax.broadcasted_iota(jnp.int32, (t, LANES), 1)
    o_ref[...] = jnp.where(lane < half, acc0 / jnp.sum(rsum0, axis=1, keepdims=True),
                           acc1 / jnp.sum(rsum1, axis=1, keepdims=True)).astype(o_ref.dtype)


def moba_attn(qa, ka, vb, bias, kv_group):
    batch, nh, s, _ = qa.shape
    npair = nh // 2
    t = MOBA_BLOCK
    nb = s // t
    assert nb % kv_group == 0 and bias.shape[1] == nb + kv_group - 1
    return pl.pallas_call(
        functools.partial(_moba_attn_kernel, kv_group=kv_group),
        grid=(npair, batch, nb),
        in_specs=[pl.BlockSpec((None, 2, t, LANES), lambda p, b, i: (b, p, i, 0)),
                  pl.BlockSpec((None, 2, s, LANES), lambda p, b, i: (b, p, 0, 0)),
                  pl.BlockSpec((None, None, s, LANES), lambda p, b, i: (b, p, 0, 0)),
                  pl.BlockSpec((2,) + bias.shape[1:], lambda p, b, i: (p, 0, 0, 0),
                               pipeline_mode=pl.Buffered(1))],
        out_specs=pl.BlockSpec((t, LANES), lambda p, b, i: (b * nb + i, p)),
        out_shape=jax.ShapeDtypeStruct((batch * s, npair * LANES), BF16),
        scratch_shapes=[pltpu.VMEM((2, nb, t, t), F32)],
        compiler_params=_params("parallel", "parallel", "arbitrary"),
        name="moba_attn",
    )(qa, ka, vb, bias)


HEAD_GROUP = 8


def _sample_scores_kernel(pt_ref, qt_ref, *refs, scale, pages_per_block):
    page_refs, o_ref = refs[:-1], refs[-1]
    nh, _, ps = page_refs[0].shape
    for h in range(nh):
        qc = qt_ref[:, h:h + 1] * scale
        for i, page in enumerate(page_refs):
            blk, p = divmod(i, pages_per_block)
            o_ref[blk, h:h + 1, p * ps:(p + 1) * ps] = jnp.sum(qc * page[h], axis=0, keepdims=True)


def sample_scores(qt, cache_t, layer, page_table, nbc):
    bsz = page_table.shape[0]
    _, _, nh, dh, ps = cache_t.shape
    ppb = MOBA_BLOCK // ps
    assert MOBA_BLOCK == ppb * ps
    bps = 2 if nbc % 2 == 0 else 1
    page = lambda i: pl.BlockSpec((None, None, nh, dh, ps),
                                  lambda b, j, pt: (layer, pt[b, j * bps * ppb + i], 0, 0, 0))
    return pl.pallas_call(
        functools.partial(_sample_scores_kernel, scale=dh ** -0.5, pages_per_block=ppb),
        grid_spec=pltpu.PrefetchScalarGridSpec(
            num_scalar_prefetch=1,
            grid=(bsz, nbc // bps),
            in_specs=[pl.BlockSpec((None, dh, nh), lambda b, j, pt: (b, 0, 0))]
                     + [page(i) for i in range(bps * ppb)],
            out_specs=pl.BlockSpec((None, bps, nh, MOBA_BLOCK), lambda b, j, pt: (b, j, 0, 0))),
        out_shape=jax.ShapeDtypeStruct((bsz, nbc, nh, MOBA_BLOCK), F32),
        compiler_params=_params("parallel", "arbitrary"),
        name="sample_scores",
    )(page_table, qt, *([cache_t] * (bps * ppb)))


def _sample_select_kernel(s_ref, sel_ref, *, k_sel, inv):
    nb, nh, _ = s_ref.shape
    gate = jnp.sum(s_ref[...], axis=-1) * inv
    n_avail = jnp.full((1, nh), nb, jnp.int32)
    _, picks = _top_k_rows(gate, n_avail, k_sel)
    for r, idx in enumerate(picks):
        sel_ref[r:r + 1, :] = idx.astype(jnp.int32)


def sample_select(scores, k_sel, scale):
    bsz, nb, nh, t = scores.shape
    return pl.pallas_call(
        functools.partial(_sample_select_kernel, k_sel=k_sel, inv=1.0 / (t * scale)),
        grid=(bsz,),
        in_specs=[pl.BlockSpec((None, nb, nh, t), lambda b: (b, 0, 0, 0))],
        out_specs=pl.BlockSpec((None, k_sel, nh), lambda b: (b, 0, 0)),
        out_shape=jax.ShapeDtypeStruct((bsz, k_sel, nh), jnp.int32),
        compiler_params=_params("parallel"),
        name="sample_select",
    )(scores)


def _sample_attn_kernel(pg_ref, sel_ref, rel_ref, q_ref, kn_ref, vn_ref, *refs, nh, k_sel, past):
    ppb = (len(refs) - 1 - k_sel) // k_sel
    s_refs, v_refs, o_ref = refs[:k_sel], refs[k_sel:k_sel + ppb * k_sel], refs[-1]
    b, h = pl.program_id(0), pl.program_id(1)
    dh, ps = v_refs[0].shape
    t = s_refs[0].shape[1]
    hl = h % q_ref.shape[0]
    q = q_ref[pl.ds(hl, 1), :] * (dh ** -0.5)
    lane = lax.broadcasted_iota(jnp.int32, (1, t), 1)
    scores = []
    for r in range(k_sel):
        blk = sel_ref[(b * nh + h) * k_sel + r]
        dist = past - (blk * MOBA_BLOCK + lane)
        scores.append(s_refs[r][pl.ds(h, 1), :] + _bias_lookup(_rel_bucket(dist), rel_ref, h))
    s_self = jnp.sum(q * kn_ref[pl.ds(hl, 1), :], axis=1, keepdims=True) + rel_ref[0, h]
    mx = s_self
    for s in scores:
        mx = jnp.maximum(mx, jnp.max(s, axis=1, keepdims=True))
    p_self = jnp.exp(s_self - mx)
    den = p_self
    acc = p_self * vn_ref[pl.ds(hl, 1), :]
    for r, s in enumerate(scores):
        p = jnp.exp(s - mx)
        den = den + jnp.sum(p, axis=1, keepdims=True)
        for i in range(ppb):
            p8 = jnp.broadcast_to(p[:, i * ps:(i + 1) * ps], (8, ps))
            acc = acc + _dot_f32(p8, v_refs[r * ppb + i][...], _dot_nt)[0:1, :]
    o_ref[...] = acc / den


def sample_attn(pages, sel_flat, rel_bias, q, k_new, v_new, scores, cache_vt, layer, k_sel, past):
    bsz, nh, dh = q.shape
    ps = cache_vt.shape[4]
    t = scores.shape[3]
    ppb = t // ps
    hg = HEAD_GROUP
    npg = ppb * k_sel

    def page(i):
        return pl.BlockSpec((None, None, None, dh, ps),
                            lambda b, h, pg, sel: (layer, pg[(b * nh + h) * npg + i], h, 0, 0))

    def picked(r):
        return pl.BlockSpec((None, None, nh, t), lambda b, h, pg, sel: (b, sel[(b * nh + h) * k_sel + r], 0, 0))

    new = lambda: pl.BlockSpec((None, hg, dh), lambda b, h, pg, sel: (b, h // hg, 0))
    return pl.pallas_call(
        functools.partial(_sample_attn_kernel, nh=nh, k_sel=k_sel, past=past),
        grid_spec=pltpu.PrefetchScalarGridSpec(
            num_scalar_prefetch=2,
            grid=(bsz, nh),
            in_specs=[pl.BlockSpec(memory_space=pltpu.SMEM), new(), new(), new()]
                     + [picked(r) for r in range(k_sel)] + [page(i) for i in range(npg)],
            out_specs=pl.BlockSpec((None, None, 1, dh), lambda b, h, pg, sel: (b, h, 0, 0))),
        out_shape=jax.ShapeDtypeStruct((bsz, nh, 1, dh), F32),
        compiler_params=_params("parallel", "arbitrary"),
        name="sample_attn",
    )(pages, sel_flat, rel_bias, q, k_new, v_new, *([scores] * k_sel), *([cache_vt] * npg))


def _mlstm_layer(hp, hs, batch, st_c, st_n, st_m, g_mix, w_in, b_gates, g_head, w_out):
    nh = ML_HEADS
    d = hp.shape[1]
    vw = w_out.shape[0]
    dv = vw // nh
    dqk = dv // 2
    qkw = nh * dqk
    w_main = jnp.concatenate([w_in[:, :qkw], w_in[:, 2 * qkw:2 * qkw + 2 * vw]], axis=1).astype(BF16)
    wkt = w_in[:, qkw:2 * qkw].T.astype(BF16)
    wgt = w_in[:, 2 * qkw + 2 * vw:].T
    w_out_b = w_out.astype(BF16)

    q, v, o, kt, gt = ml_proj(hp, g_mix, w_main, wkt, wgt, b_gates)
    hsp, c_p, n_p, m_p = mlstm_prompt(q, v, o, kt, gt, g_head, batch)
    hp = out_proj(hsp, w_out_b, hp)

    bs = hs.shape[0]
    q, v, o, kt, gt = ml_proj(hs, g_mix, w_main, wkt, wgt, b_gates)
    hss, c_s, n_s, m_s = mlstm_sample(q.reshape(bs, nh, dqk), kt.T.reshape(bs, nh, dqk), v.reshape(bs, nh, dv),
                                      o.reshape(bs, nh, dv), gt.T.reshape(bs, 1, 2 * nh), g_head,
                                      st_c, st_n, st_m.reshape(bs, 1, nh))
    hs = out_proj(hss.reshape(bs, vw), w_out_b, hs)
    return hp, hs, (c_p, n_p, m_p), (c_s, n_s, m_s.reshape(bs, nh))


def _moba_layer(hp, hs, batch, cache_kt, cache_vt, layer, page_table, bias, kv_group, rel_bias, g_mix, w_qkv,
                w_out, kbuf, vbuf):
    _, _, nh, dh, ps = cache_kt.shape
    w = w_qkv.shape[1] // 3
    assert dh == LANES // 2 and w == nh * dh
    w_qkv_b = w_qkv.astype(BF16)
    w_out_b = w_out.astype(BF16)

    q, k, v = norm_proj(hp, g_mix, w_qkv_b, (w, w, w))
    kmeans = block_means(k)
    qa, ka, vb, kbuf, vbuf = moba_select(q, k, v, kmeans, batch, layer, kbuf, vbuf)
    o = moba_attn(qa, ka, vb, bias, kv_group)
    hp = out_proj(o, w_out_b, hp)

    bs, n_pages = page_table.shape
    past = n_pages * ps
    nbc = past // MOBA_BLOCK
    assert past == nbc * MOBA_BLOCK and nbc >= 1, "a partially filled tail block is not supported"
    k_sel = min(MOBA_TOPK, nbc)
    (qkv_s,) = norm_proj(hs, g_mix, w_qkv_b, (3 * w,))
    q_s, k_s, v_s = (qkv_s[:, i * w:(i + 1) * w].reshape(bs, nh, dh) for i in range(3))
    ppb = MOBA_BLOCK // ps
    scores = sample_scores(jnp.transpose(q_s, (0, 2, 1)), cache_kt, layer, page_table, nbc)
    sel = sample_select(scores, k_sel, dh ** -0.5)
    sel_bhk = jnp.transpose(sel, (0, 2, 1))
    logical = sel_bhk[..., None] * ppb + jnp.arange(ppb, dtype=jnp.int32)
    pages = jnp.take_along_axis(page_table, logical.reshape(bs, -1), axis=1)
    o_s = sample_attn(pages.reshape(-1), sel_bhk.reshape(-1), rel_bias, q_s, k_s, v_s,
                      scores, cache_vt, layer, k_sel, past)
    hs = out_proj(o_s.reshape(bs, w), w_out_b, hs)
    return hp, hs, kbuf, vbuf, (k_s, v_s)


def kernel(x_prompt, x_sample, cache_k, cache_v, state_C, state_n, state_m, page_table, norm_mix, norm_ffn,
           norm_final, rel_bias, w_ml_in, b_ml_gates, g_ml_head, w_ml_out, w_attn_qkv, w_attn_out, w_ffn_in,
           w_ffn_out):
    batch, seq, d = x_prompt.shape
    bs, dec_seq, _ = x_sample.shape
    assert dec_seq == 1 and seq % MOBA_BLOCK == 0
    depth = norm_mix.shape[0]
    nh_at = cache_k.shape[3]
    dh = cache_k.shape[4]
    hp = x_prompt.reshape(batch * seq, d)
    hs = x_sample.reshape(bs, d)
    w_ffn_in_b = w_ffn_in.astype(BF16)
    w_ffn_out_b = w_ffn_out.astype(BF16)
    nb = seq // MOBA_BLOCK
    kv_group = math.gcd(nb, KV_GROUP)
    bias = bias_tiles(rel_bias, 1 - kv_group, nb - 1, MOBA_BLOCK)
    cache_kt = jnp.transpose(cache_k, (0, 1, 3, 4, 2))
    cache_vt = jnp.transpose(cache_v, (0, 1, 3, 4, 2))

    kbuf = jnp.zeros((depth // 2, batch, nh_at * dh, seq), F32)
    vbuf = jnp.zeros((depth // 2, batch, nh_at * dh, seq), F32)
    kv_s, st_p, st_s = [], [], []
    for layer in range(depth):
        j = layer // 2
        if layer % 2 == 0:
            hp, hs, sp, ss = _mlstm_layer(hp, hs, batch, state_C[j], state_n[j], state_m[j], norm_mix[layer],
                                          w_ml_in[j], b_ml_gates[j], g_ml_head[j], w_ml_out[j])
            st_p.append(sp)
            st_s.append(ss)
        else:
            hp, hs, kbuf, vbuf, ks = _moba_layer(hp, hs, batch, cache_kt, cache_vt, j, page_table, bias, kv_group,
                                                 rel_bias, norm_mix[layer], w_attn_qkv[j], w_attn_out[j],
                                                 kbuf, vbuf)
            kv_s.append(ks)
        hp = ffn(hp, norm_ffn[layer], w_ffn_in_b[layer], w_ffn_out_b[layer])
        hs = ffn(hs, norm_ffn[layer], w_ffn_in_b[layer], w_ffn_out_b[layer])

    y_prompt = rmsnorm(hp, norm_final).reshape(batch, seq, d)
    y_sample = rmsnorm(hs, norm_final).reshape(bs, dec_seq, d)
    stack = lambda items, i, shape: jnp.stack([it[i] for it in items]).reshape((len(items),) + shape)
    to_rows = lambda buf: jnp.transpose(buf.reshape(buf.shape[0], batch, nh_at, dh, seq), (0, 1, 4, 2, 3))
    return (y_prompt, y_sample, to_rows(kbuf), to_rows(vbuf),
            stack(kv_s, 0, (bs, dec_seq, nh_at, dh)), stack(kv_s, 1, (bs, dec_seq, nh_at, dh)),
            jnp.stack([s[0] for s in st_p]), jnp.stack([s[1] for s in st_p]), jnp.stack([s[2] for s in st_p]),
            jnp.stack([s[0] for s in st_s]), jnp.stack([s[1] for s in st_s]), jnp.stack([s[2] for s in st_s]))
```

```python
import functools
import math

import jax
import jax.numpy as jnp
from jax import lax
from jax.experimental import pallas as pl
from jax.experimental.pallas import tpu as pltpu

F32 = jnp.float32
BF16 = jnp.bfloat16

EPS = 1e-6
ML_HEADS = 8
ML_CHUNK = 256
MOBA_BLOCK = 256
MOBA_TOPK = 3
KV_GROUP = 4
SUM_ROWS = 16
LOG2E = 1.4426950408889634
NUM_BUCKETS = 32
MAX_DISTANCE = 4096
NEG_BIG = -1e30
LANES = 128
VMEM_LIMIT = 56 * 1024 * 1024


def _dot(a, b):
    return jnp.dot(a, b, preferred_element_type=F32)


def _dot_nt(a, b):
    return lax.dot_general(a, b, (((1,), (1,)), ((), ())), preferred_element_type=F32)


def _split2(x):
    hi = x.astype(BF16)
    lo = (x - hi.astype(F32)).astype(BF16)
    return hi, lo


def _split3(x):
    hi = x.astype(BF16)
    r = x - hi.astype(F32)
    mid = r.astype(BF16)
    lo = (r - mid.astype(F32)).astype(BF16)
    return hi, mid, lo


def _dot_f32(a, b, dot=_dot):
    ah, al = _split2(a)
    bh, bl = _split2(b)
    return dot(ah, bh) + dot(ah, bl) + dot(al, bh)


def _norm_rows(x, g):
    return x * lax.rsqrt(jnp.mean(x * x, axis=-1, keepdims=True) + EPS) * g


def _log_sigmoid(x):
    return jnp.minimum(x, 0.0) - jnp.log1p(jnp.exp(-jnp.abs(x)))


def _rel_bucket(dist):
    n = jnp.maximum(dist, 0)
    max_exact = NUM_BUCKETS // 2
    nf = jnp.maximum(n, 1).astype(F32)
    large = max_exact + (jnp.log(nf / max_exact) / math.log(MAX_DISTANCE / max_exact)
                         * (NUM_BUCKETS - max_exact)).astype(jnp.int32)
    return jnp.where(n < max_exact, n, jnp.minimum(large, NUM_BUCKETS - 1))


def _bias_lookup(bucket, rel_ref, h):
    acc = jnp.full(bucket.shape, rel_ref[0, h], F32)
    for k in range(1, NUM_BUCKETS):
        acc = jnp.where(bucket == k, rel_ref[k, h], acc)
    return acc


def _params(*sem):
    return pltpu.CompilerParams(dimension_semantics=sem, vmem_limit_bytes=VMEM_LIMIT)


def _row_tile(m, target):
    t = min(m, target)
    assert m % t == 0, (m, t)
    return t


def _proj_kernel(x_ref, g_ref, w_ref, *o_refs):
    xb = _norm_rows(x_ref[...], g_ref[...]).astype(BF16)
    c0 = 0
    for o_ref in o_refs:
        n = o_ref.shape[1]
        for c in range(0, n, 512):
            w = min(512, n - c)
            o_ref[:, c:c + w] = _dot(xb, w_ref[:, c0 + c:c0 + c + w])
        c0 += n


def norm_proj(x, g, w, widths, tm=512):
    m, d = x.shape
    tm = _row_tile(m, tm)
    n = w.shape[1]
    assert sum(widths) == n
    return pl.pallas_call(
        _proj_kernel,
        grid=(m // tm,),
        in_specs=[pl.BlockSpec((tm, d), lambda i: (i, 0)),
                  pl.BlockSpec((1, d), lambda i: (0, 0)),
                  pl.BlockSpec((d, n), lambda i: (0, 0))],
        out_specs=[pl.BlockSpec((tm, wd), lambda i: (i, 0)) for wd in widths],
        out_shape=[jax.ShapeDtypeStruct((m, wd), F32) for wd in widths],
        compiler_params=_params("parallel"),
        name="norm_proj",
    )(x, g.reshape(1, d), w)


def _ml_proj_kernel(x_ref, g_ref, w_ref, wkt_ref, wgt_ref, bg_ref, q_ref, v_ref, o_ref, kt_ref, gt_ref):
    xn = _norm_rows(x_ref[...], g_ref[...])
    xb = xn.astype(BF16)
    xlo = (xn - xb.astype(F32)).astype(BF16)
    c0 = 0
    for out in (q_ref, v_ref, o_ref):
        n = out.shape[1]
        for c in range(0, n, 512):
            out[:, c:c + 512] = _dot(xb, w_ref[:, c0 + c:c0 + c + 512])
        c0 += n
    kt_ref[...] = _dot_nt(wkt_ref[...], xb)
    gh, gl = _split2(wgt_ref[...])
    gt_ref[...] = _dot_nt(gh, xb) + _dot_nt(gh, xlo) + _dot_nt(gl, xb) + bg_ref[...]


def ml_proj(x, g, w_main, wkt, wgt, bg, tm=512):
    m, d = x.shape
    tm = _row_tile(m, tm)
    qk = wkt.shape[0]
    vw = (w_main.shape[1] - qk) // 2
    ng = wgt.shape[0]
    return pl.pallas_call(
        _ml_proj_kernel,
        grid=(m // tm,),
        in_specs=[pl.BlockSpec((tm, d), lambda i: (i, 0)),
                  pl.BlockSpec((1, d), lambda i: (0, 0)),
                  pl.BlockSpec(w_main.shape, lambda i: (0, 0)),
                  pl.BlockSpec(wkt.shape, lambda i: (0, 0)),
                  pl.BlockSpec(wgt.shape, lambda i: (0, 0)),
                  pl.BlockSpec((ng, 1), lambda i: (0, 0))],
        out_specs=[pl.BlockSpec((tm, qk), lambda i: (i, 0)),
                   pl.BlockSpec((tm, vw), lambda i: (i, 0)),
                   pl.BlockSpec((tm, vw), lambda i: (i, 0)),
                   pl.BlockSpec((qk, tm), lambda i: (0, i)),
                   pl.BlockSpec((ng, tm), lambda i: (0, i))],
        out_shape=[jax.ShapeDtypeStruct((m, qk), F32),
                   jax.ShapeDtypeStruct((m, vw), F32),
                   jax.ShapeDtypeStruct((m, vw), F32),
                   jax.ShapeDtypeStruct((qk, m), F32),
                   jax.ShapeDtypeStruct((ng, m), F32)],
        compiler_params=_params("parallel"),
        name="ml_proj",
    )(x, g.reshape(1, d), w_main, wkt, wgt, bg.reshape(ng, 1))


def _out_proj_kernel(a_ref, w_ref, r_ref, o_ref):
    o_ref[...] = r_ref[...] + _dot(a_ref[...].astype(BF16), w_ref[...])


def out_proj(a, w, res, tm=1024):
    m, k = a.shape
    d = w.shape[1]
    tm = _row_tile(m, tm)
    return pl.pallas_call(
        _out_proj_kernel,
        grid=(m // tm,),
        in_specs=[pl.BlockSpec((tm, k), lambda i: (i, 0)),
                  pl.BlockSpec((k, d), lambda i: (0, 0)),
                  pl.BlockSpec((tm, d), lambda i: (i, 0))],
        out_specs=pl.BlockSpec((tm, d), lambda i: (i, 0)),
        out_shape=jax.ShapeDtypeStruct((m, d), F32),
        compiler_params=_params("parallel"),
        name="out_proj",
    )(a, w, res)


def _ffn_kernel(x_ref, g_ref, wg_ref, wu_ref, wo_ref, o_ref, xn_s, acc_s):
    f = pl.program_id(1)

    @pl.when(f == 0)
    def _():
        xn_s[...] = _norm_rows(x_ref[...], g_ref[...]).astype(BF16)
        acc_s[...] = jnp.zeros_like(acc_s)

    xb = xn_s[...]
    gp = _dot(xb, wg_ref[...])
    up = _dot(xb, wu_ref[...])
    act = (gp * jax.nn.sigmoid(gp) * up).astype(BF16)
    acc_s[...] += _dot(act, wo_ref[...])

    @pl.when(f == pl.num_programs(1) - 1)
    def _():
        o_ref[...] = x_ref[...] + acc_s[...]


def ffn(x, g, w_in, w_out, tm=1024, tf=256):
    m, d = x.shape
    dff = w_out.shape[0]
    tm = _row_tile(m, tm)
    nf = dff // tf
    assert dff % tf == 0
    return pl.pallas_call(
        _ffn_kernel,
        grid=(m // tm, nf),
        in_specs=[pl.BlockSpec((tm, d), lambda i, f: (i, 0)),
                  pl.BlockSpec((1, d), lambda i, f: (0, 0)),
                  pl.BlockSpec((d, tf), lambda i, f: (0, f)),
                  pl.BlockSpec((d, tf), lambda i, f: (0, nf + f)),
                  pl.BlockSpec((tf, d), lambda i, f: (f, 0))],
        out_specs=pl.BlockSpec((tm, d), lambda i, f: (i, 0)),
        out_shape=jax.ShapeDtypeStruct((m, d), F32),
        scratch_shapes=[pltpu.VMEM((tm, d), BF16), pltpu.VMEM((tm, d), F32)],
        compiler_params=_params("parallel", "arbitrary"),
        name="ffn",
    )(x, g.reshape(1, d), w_in, w_in, w_out)


def _rmsnorm_kernel(x_ref, g_ref, o_ref):
    o_ref[...] = _norm_rows(x_ref[...], g_ref[...])


def rmsnorm(x, g, tm=1024):
    m, d = x.shape
    tm = _row_tile(m, tm)
    return pl.pallas_call(
        _rmsnorm_kernel,
        grid=(m // tm,),
        in_specs=[pl.BlockSpec((tm, d), lambda i: (i, 0)), pl.BlockSpec((1, d), lambda i: (0, 0))],
        out_specs=pl.BlockSpec((tm, d), lambda i: (i, 0)),
        out_shape=jax.ShapeDtypeStruct((m, d), F32),
        compiler_params=_params("parallel"),
        name="rmsnorm",
    )(x, g.reshape(1, d))


def _mlstm_chunk_kernel(q_ref, v_ref, o_ref, kt_ref, gt_ref, gh_ref, hs_ref, caug_ref, m_ref, caug_s, m_s):
    c = pl.program_id(1)
    L = q_ref.shape[0]
    nh = caug_s.shape[0]
    dqk = caug_s.shape[1]
    dv = caug_s.shape[2] // 2

    @pl.when(c == 0)
    def _():
        caug_s[...] = jnp.zeros_like(caug_s)
        m_s[...] = jnp.zeros_like(m_s)

    row = lax.broadcasted_iota(jnp.int32, (L, L), 0)
    col = lax.broadcasted_iota(jnp.int32, (L, L), 1)
    causal = col <= row
    lower = jnp.where(causal, 1.0, 0.0).astype(BF16)
    upper = jnp.where(row <= col, 1.0, 0.0).astype(BF16)
    gt = gt_ref[...]
    ig = gt[0:nh, :]
    logf = _log_sigmoid(gt[nh:2 * nh, :])
    parts = _split3(logf)
    f_row = _dot(parts[0], upper) + _dot(parts[1], upper) + _dot(parts[2], upper)
    f_col = _dot_nt(lower, parts[0]) + _dot_nt(lower, parts[1]) + _dot_nt(lower, parts[2])
    a_rows = ig - f_row
    ones_col = jnp.where(lax.broadcasted_iota(jnp.int32, (L, dv), 1) == 0, 1.0, 0.0).astype(BF16)

    for h in range(nh):
        m_prev = m_s[h:h + 1, 0:1]
        a_row = a_rows[h:h + 1, :]
        amat = jnp.where(causal, a_row, -jnp.inf)
        g = jnp.maximum(jnp.max(amat, axis=1, keepdims=True), m_prev)
        w = jnp.exp(amat - g)
        qh = q_ref[:, h * dqk:(h + 1) * dqk].astype(BF16)
        kth = kt_ref[h * dqk:(h + 1) * dqk, :] * (dqk ** -0.5)
        p = (_dot(qh, kth.astype(BF16)) * w).astype(BF16)
        vaug = jnp.concatenate([v_ref[:, h * dv:(h + 1) * dv].astype(BF16), ones_col], axis=1)
        caug = caug_s[h]
        a_int = jnp.exp(m_prev - g)
        tot = a_int * _dot(qh, caug.astype(BF16)) + _dot(p, vaug)
        m_t = f_col[:, h:h + 1] + g
        hh = tot[:, :dv] / jnp.maximum(jnp.abs(tot[:, dv:dv + 1]), jnp.exp(-m_t))
        hn = hh * lax.rsqrt(jnp.mean(hh * hh, axis=-1, keepdims=True) + EPS) * gh_ref[:, h * dv:(h + 1) * dv]
        hs_ref[:, h * dv:(h + 1) * dv] = (hn * jax.nn.sigmoid(o_ref[:, h * dv:(h + 1) * dv])).astype(hs_ref.dtype)
        g_last = g[L - 1:L, :]
        w_last = jnp.exp(a_row - g_last)
        caug_s[h] = jnp.exp(m_prev - g_last) * caug + _dot((kth * w_last).astype(BF16), vaug)
        m_s[h:h + 1, :] = jnp.broadcast_to(f_row[h:h + 1, L - 1:L] + g_last, (1, m_s.shape[1]))

    @pl.when(c == pl.num_programs(1) - 1)
    def _():
        caug_ref[...] = caug_s[...]
        m_ref[...] = m_s[...]


def mlstm_prompt(q, v, o, kt, gt, g_head, batch):
    m, qkw = q.shape
    vw = v.shape[1]
    nh = gt.shape[0] // 2
    dqk, dv = qkw // nh, vw // nh
    s = m // batch
    L = math.gcd(s, ML_CHUNK)
    nc = s // L
    hs, caug, mm = pl.pallas_call(
        _mlstm_chunk_kernel,
        grid=(batch, nc),
        in_specs=[pl.BlockSpec((L, qkw), lambda b, c: (b * nc + c, 0)),
                  pl.BlockSpec((L, vw), lambda b, c: (b * nc + c, 0)),
                  pl.BlockSpec((L, vw), lambda b, c: (b * nc + c, 0)),
                  pl.BlockSpec((qkw, L), lambda b, c: (0, b * nc + c)),
                  pl.BlockSpec((2 * nh, L), lambda b, c: (0, b * nc + c)),
                  pl.BlockSpec((1, vw), lambda b, c: (0, 0))],
        out_specs=[pl.BlockSpec((L, vw), lambda b, c: (b * nc + c, 0)),
                   pl.BlockSpec((None, nh, dqk, 2 * dv), lambda b, c: (b, 0, 0, 0)),
                   pl.BlockSpec((None, nh, LANES), lambda b, c: (b, 0, 0))],
        out_shape=[jax.ShapeDtypeStruct((m, vw), BF16),
                   jax.ShapeDtypeStruct((batch, nh, dqk, 2 * dv), F32),
                   jax.ShapeDtypeStruct((batch, nh, LANES), F32)],
        scratch_shapes=[pltpu.VMEM((nh, dqk, 2 * dv), F32), pltpu.VMEM((nh, LANES), F32)],
        compiler_params=_params("parallel", "arbitrary"),
        name="mlstm_prompt",
    )(q, v, o, kt, gt, g_head.reshape(1, vw))
    return hs, caug[..., :dv], caug[..., dv], mm[..., 0]


def _mlstm_step_kernel(q_ref, k_ref, v_ref, o_ref, gt_ref, m_ref, gh_ref, c_ref, n_ref,
                       hs_ref, cn_ref, nn_ref, mn_ref):
    nh, dqk, dv = c_ref.shape
    scale = dqk ** -0.5
    q = q_ref[...]
    k = k_ref[...] * scale
    eye = jnp.where(lax.broadcasted_iota(jnp.int32, (dqk, dqk), 0)
                    == lax.broadcasted_iota(jnp.int32, (dqk, dqk), 1), 1.0, 0.0).astype(BF16)
    q3, k3 = _split3(q), _split3(k)
    q_cols = _dot_nt(eye, q3[0]) + _dot_nt(eye, q3[1]) + _dot_nt(eye, q3[2])
    k_cols = _dot_nt(eye, k3[0]) + _dot_nt(eye, k3[1]) + _dot_nt(eye, k3[2])
    gt = gt_ref[...]
    for h in range(nh):
        ig = gt[:, h:h + 1]
        logf = _log_sigmoid(gt[:, nh + h:nh + h + 1])
        m_inter = m_ref[:, h:h + 1] + logf
        m_t = jnp.maximum(m_inter, ig)
        w = jnp.exp(ig - m_t)
        a = jnp.exp(m_inter - m_t)
        q_row, k_row = q[h:h + 1, :], k[h:h + 1, :]
        q_col, k_col = q_cols[:, h:h + 1], k_cols[:, h:h + 1]
        v_row = v_ref[h:h + 1, :]
        c_h = c_ref[h]
        n_row = n_ref[h:h + 1, :]
        sqk = jnp.sum(q_row * k_row, axis=1, keepdims=True) * w
        num = a * jnp.sum(q_col * c_h, axis=0, keepdims=True) + sqk * v_row
        den = a * jnp.sum(q_row * n_row, axis=1, keepdims=True) + sqk
        hh = num / jnp.maximum(jnp.abs(den), jnp.exp(-m_t))
        hn = hh * lax.rsqrt(jnp.mean(hh * hh, axis=-1, keepdims=True) + EPS) * gh_ref[h:h + 1, :]
        hs_ref[h:h + 1, :] = (hn * jax.nn.sigmoid(o_ref[h:h + 1, :])).astype(hs_ref.dtype)
        cn_ref[h] = a * c_h + (w * k_col) * v_row
        nn_ref[h:h + 1, :] = a * n_row + w * k_row
        mn_ref[:, h:h + 1] = m_t


def mlstm_sample(q, k, v, o, gates, g_head, c0, n0, m0):
    bsz, nh, dqk, dv = c0.shape
    row3 = lambda w: pl.BlockSpec((None, nh, w), lambda b: (b, 0, 0))
    return pl.pallas_call(
        _mlstm_step_kernel,
        grid=(bsz,),
        in_specs=[row3(dqk), row3(dqk), row3(dv), row3(dv),
                  pl.BlockSpec((None, 1, 2 * nh), lambda b: (b, 0, 0)),
                  pl.BlockSpec((None, 1, nh), lambda b: (b, 0, 0)),
                  pl.BlockSpec((nh, dv), lambda b: (0, 0)),
                  pl.BlockSpec((None, nh, dqk, dv), lambda b: (b, 0, 0, 0)),
                  row3(dqk)],
        out_specs=[row3(dv),
                   pl.BlockSpec((None, nh, dqk, dv), lambda b: (b, 0, 0, 0)),
                   row3(dqk),
                   pl.BlockSpec((None, 1, nh), lambda b: (b, 0, 0))],
        out_shape=[jax.ShapeDtypeStruct((bsz, nh, dv), F32),
                   jax.ShapeDtypeStruct((bsz, nh, dqk, dv), F32),
                   jax.ShapeDtypeStruct((bsz, nh, dqk), F32),
                   jax.ShapeDtypeStruct((bsz, 1, nh), F32)],
        compiler_params=_params("parallel"),
        name="mlstm_sample",
    )(q, k, v, o, gates, m0, g_head.reshape(nh, dv), c0, n0)


def _bucket_bounds(lo, hi):
    def bucket(n):
        n = max(n, 0)
        max_exact = NUM_BUCKETS // 2
        if n < max_exact:
            return n
        return min(max_exact + int(math.log(n / max_exact) / math.log(MAX_DISTANCE / max_exact)
                                   * (NUM_BUCKETS - max_exact)), NUM_BUCKETS - 1)
    return max(bucket(lo) - 1, 0), min(bucket(hi) + 1, NUM_BUCKETS - 1)


def _bias_tiles_kernel(kr_ref, rel_ref, o_ref, *, d_min):
    step = pl.program_id(0)
    d = step + d_min
    nh, t, _ = o_ref.shape
    dist = d * t + lax.broadcasted_iota(jnp.int32, (t, t), 1) - lax.broadcasted_iota(jnp.int32, (t, t), 0)
    bucket = _rel_bucket(dist)
    k_lo, k_hi = kr_ref[2 * step], kr_ref[2 * step + 1]
    for h in range(nh):
        acc = lax.fori_loop(k_lo + 1, k_hi + 1, lambda k, a: jnp.where(bucket == k, rel_ref[k, h], a),
                            jnp.full((t, t), rel_ref[k_lo, h], F32))
        o_ref[h] = jnp.where(dist >= 0, acc, NEG_BIG)


def bias_tiles(rel_bias, d_min, d_max, t):
    nh = rel_bias.shape[1]
    nd = d_max - d_min + 1
    bounds = [b for d in range(d_min, d_max + 1) for b in _bucket_bounds(d * t - (t - 1), d * t + (t - 1))]
    return pl.pallas_call(
        functools.partial(_bias_tiles_kernel, d_min=d_min),
        grid_spec=pltpu.PrefetchScalarGridSpec(
            num_scalar_prefetch=1,
            grid=(nd,),
            in_specs=[pl.BlockSpec(memory_space=pltpu.SMEM)],
            out_specs=pl.BlockSpec((nh, None, t, t), lambda d, kr: (0, d, 0, 0))),
        out_shape=jax.ShapeDtypeStruct((nh, nd, t, t), F32),
        compiler_params=_params("parallel"),
        name="bias_tiles",
    )(jnp.asarray(bounds, jnp.int32), rel_bias)


def _kmeans_kernel(k_ref, o_ref):
    nblk = o_ref.shape[0]
    t = k_ref.shape[0] // nblk
    for j in range(nblk):
        o_ref[j:j + 1, :] = jnp.sum(k_ref[j * t:(j + 1) * t, :], axis=0, keepdims=True) * (1.0 / t)


def block_means(k, blocks_per_step=8):
    m, w = k.shape
    rows = MOBA_BLOCK * blocks_per_step
    assert m % rows == 0
    return pl.pallas_call(
        _kmeans_kernel,
        grid=(m // rows,),
        in_specs=[pl.BlockSpec((rows, w), lambda i: (i, 0))],
        out_specs=pl.BlockSpec((blocks_per_step, w), lambda i: (i, 0)),
        out_shape=jax.ShapeDtypeStruct((m // MOBA_BLOCK, w), F32),
        compiler_params=_params("parallel"),
        name="block_means",
    )(k)


def _top_k_rows(gate, n_avail, k_sel):
    nb = gate.shape[0]
    jidx = lax.broadcasted_iota(jnp.int32, gate.shape, 0).astype(F32)
    selected = jnp.zeros(gate.shape, F32)
    picks = []
    for r in range(k_sel):
        mx = jnp.max(gate, axis=0, keepdims=True)
        idx = jnp.min(jnp.where(gate == mx, jidx, float(nb)), axis=0, keepdims=True)
        hit = jidx == idx
        selected = jnp.where(jnp.logical_and(hit, r < n_avail), 1.0, selected)
        gate = jnp.where(hit, -jnp.inf, gate)
        picks.append(idx)
    return selected > 0.5, picks


def _moba_select_kernel(q_ref, k_ref, v_ref, km_ref, kbuf_ref, vbuf_ref, qa_ref, ka_ref, va_ref, kt_ref, vt_ref,
                        *, nb, k_sel):
    del kbuf_ref, vbuf_ref
    t = pl.program_id(2)
    tq = q_ref.shape[0]
    half = LANES // 2
    shift = MOBA_BLOCK.bit_length() - 1
    q2, k2 = q_ref[...], k_ref[...]
    qt = q2.T * (half ** -0.5)
    vt = v_ref[...].T
    kt_ref[...] = k2.T
    vt_ref[...] = vt
    ngrp, _, gw = va_ref.shape
    va = jnp.concatenate([vt.astype(BF16), jnp.ones((SUM_ROWS, tq), BF16)], axis=0)
    for gi in range(ngrp):
        va_ref[gi] = va[:, gi * gw:(gi + 1) * gw]
    km = km_ref[...]
    lane = lax.broadcasted_iota(jnp.int32, (tq, LANES), 1)
    rowblk = (t * tq + lax.broadcasted_iota(jnp.int32, (tq, LANES), 0)) >> shift
    feat = lax.broadcasted_iota(jnp.int32, (LANES, tq), 0)
    km_lane = lax.broadcasted_iota(jnp.int32, km.shape, 1)
    pos = t * tq + lax.broadcasted_iota(jnp.int32, (1, tq), 1)
    n_avail = jnp.minimum(pos >> shift, nb)
    jidx = lax.broadcasted_iota(jnp.int32, (nb, tq), 0)
    for hh in range(2):
        off = 0 if hh else half
        kmh = jnp.where((km_lane >= half) if hh else (km_lane < half), km, 0.0)
        gate = _dot_f32(kmh, q2, _dot_nt)
        gate = jnp.where(jidx < n_avail, gate, -jnp.inf)
        selected, _ = _top_k_rows(gate, n_avail, k_sel)
        code = jnp.where(jnp.logical_or(selected, jidx == (pos >> shift)), 0.0, NEG_BIG)
        pieces = [code, jnp.zeros((LANES - nb - off, tq), F32)]
        if off:
            pieces = [jnp.zeros((off, tq), F32)] + pieces
        code_rows = jnp.concatenate(pieces, axis=0)
        qa_ref[hh] = jnp.where((feat >= half) if hh else (feat < half), qt, code_rows).astype(BF16)
        onehot = jnp.where((lane - off) == rowblk, 1.0, 0.0)
        ka_ref[hh] = jnp.where((lane >= half) if hh else (lane < half), k2, onehot).astype(BF16)


def moba_select(q, k, v, kmeans, batch, slot, kbuf, vbuf, kv_group, tq=2048):
    m, w = q.shape
    s = m // batch
    nb = s // MOBA_BLOCK
    npair = w // LANES
    gw = kv_group * MOBA_BLOCK
    tq = max(min(tq, s), gw)
    nt = s // tq
    assert nb <= LANES // 2 and s % tq == 0 and tq % gw == 0 and kbuf.shape[1:] == (batch, w, s)
    k_sel = min(MOBA_TOPK, nb)
    rows = lambda: pl.BlockSpec((tq, LANES), lambda b, p, t: (b * nt + t, p))
    feat = lambda: pl.BlockSpec((None, None, LANES, tq), lambda b, p, t: (slot, b, p, t))
    return pl.pallas_call(
        functools.partial(_moba_select_kernel, nb=nb, k_sel=k_sel),
        grid=(batch, npair, nt),
        in_specs=[rows(), rows(), rows(),
                  pl.BlockSpec((nb, LANES), lambda b, p, t: (b, p)),
                  pl.BlockSpec(memory_space=pl.ANY), pl.BlockSpec(memory_space=pl.ANY)],
        out_specs=[pl.BlockSpec((None, 2, LANES, tq), lambda b, p, t: (b, p, 0, t)),
                   pl.BlockSpec((None, 2, tq, LANES), lambda b, p, t: (b, p, t, 0)),
                   pl.BlockSpec((None, None, tq // gw, LANES + SUM_ROWS, gw), lambda b, p, t: (b, p, t, 0, 0)),
                   feat(), feat()],
        out_shape=[jax.ShapeDtypeStruct((batch, 2 * npair, LANES, s), BF16),
                   jax.ShapeDtypeStruct((batch, 2 * npair, s, LANES), BF16),
                   jax.ShapeDtypeStruct((batch, npair, s // gw, LANES + SUM_ROWS, gw), BF16),
                   jax.ShapeDtypeStruct(kbuf.shape, F32),
                   jax.ShapeDtypeStruct(vbuf.shape, F32)],
        input_output_aliases={4: 3, 5: 4},
        compiler_params=_params("parallel", "parallel", "parallel"),
        name="moba_select",
    )(q, k, v, kmeans, kbuf, vbuf)


def _moba_attn_kernel(qa_ref, ka_ref, va_ref, bias_ref, o_ref, s_a, s_b, m_a, m_b, *, kv_group, d_min, d_sat):
    s = pl.program_id(2)
    nb = pl.num_programs(2) - 1
    t = qa_ref.shape[2]
    gw = kv_group * t
    half = LANES // 2
    n_cur = jnp.where(s < nb, s // kv_group + 1, 0)
    n_prev = jnp.where(s > 0, (s + kv_group - 1) // kv_group, 0)
    n_both = jnp.minimum(n_cur, n_prev)
    qs = (qa_ref[0], qa_ref[1])

    def fold(x, op):
        return op(x.reshape(x.shape[0] // 8, 8, x.shape[1]), axis=0)

    def run(s_cur, m_cur, s_prev, m_prev):
        m_old = (m_prev[0], m_prev[1])

        def scores(g, rmax):
            start = pl.multiple_of(g * gw, gw)
            rmax = list(rmax)
            for hh in range(2):
                sc = _dot(ka_ref[hh, pl.ds(start, gw), :], qs[hh])
                for jj in range(kv_group):
                    d = jnp.minimum(s - (g * kv_group + jj), d_sat)
                    sj = (sc[jj * t:(jj + 1) * t, :] + bias_ref[hh, d - d_min]) * LOG2E
                    s_cur[hh, pl.ds(pl.multiple_of(start + jj * t, t), t), :] = sj
                    rmax[hh] = jnp.maximum(rmax[hh], fold(sj, jnp.max))
            return tuple(rmax)

        def values(g, accs):
            start = pl.multiple_of(g * gw, gw)
            vgrp = va_ref[g]
            out = []
            for hh in range(2):
                p = jnp.exp2((s_prev[hh, pl.ds(start, gw), :] - m_old[hh]).astype(BF16))
                out.append(accs[hh] + _dot(vgrp, p))
            return tuple(out)

        carry = (jnp.full((8, t), NEG_BIG, F32),) * 2 + (jnp.zeros((LANES + SUM_ROWS, t), F32),) * 2
        carry = lax.fori_loop(0, n_both, lambda g, c: scores(g, c[:2]) + values(g, c[2:]), carry)
        rmax = lax.fori_loop(n_both, n_cur, scores, carry[:2])
        acc0, acc1 = lax.fori_loop(n_both, n_prev, values, carry[2:])
        for hh in range(2):
            m_cur[hh] = jnp.max(rmax[hh], axis=0, keepdims=True)

        @pl.when(s > 0)
        def _():
            feat = lax.broadcasted_iota(jnp.int32, (LANES, t), 0)
            out_t = jnp.where(feat < half, acc0[:LANES] / acc0[LANES:LANES + 1],
                              acc1[:LANES] / acc1[LANES:LANES + 1])
            o_ref[...] = out_t.T.astype(o_ref.dtype)

    @pl.when(s % 2 == 0)
    def _():
        run(s_a, m_a, s_b, m_b)

    @pl.when(s % 2 == 1)
    def _():
        run(s_b, m_b, s_a, m_a)


def _saturation_block_distance(t, nb):
    max_exact = NUM_BUCKETS // 2
    n_sat = max_exact * (MAX_DISTANCE / max_exact) ** ((NUM_BUCKETS - 1 - max_exact) / (NUM_BUCKETS - max_exact))
    return min(math.ceil((1.02 * n_sat + t - 1) / t), nb - 1)


def moba_attn(qa, ka, va, bias, kv_group, d_min, d_sat):
    batch, nh, s, _ = ka.shape
    npair = nh // 2
    t = MOBA_BLOCK
    nb = s // t
    assert nb % kv_group == 0 and d_min <= 1 - kv_group and bias.shape[1] == d_sat - d_min + 1
    last = nb - 1
    return pl.pallas_call(
        functools.partial(_moba_attn_kernel, kv_group=kv_group, d_min=d_min, d_sat=d_sat),
        grid=(npair, batch, nb + 1),
        in_specs=[pl.BlockSpec((None, 2, LANES, t), lambda p, b, i: (b, p, 0, jnp.minimum(i, last))),
                  pl.BlockSpec((None, 2, s, LANES), lambda p, b, i: (b, p, 0, 0)),
                  pl.BlockSpec((None, None) + va.shape[2:], lambda p, b, i: (b, p, 0, 0, 0),
                               pipeline_mode=pl.Buffered(1)),
                  pl.BlockSpec((2,) + bias.shape[1:], lambda p, b, i: (p, 0, 0, 0),
                               pipeline_mode=pl.Buffered(1))],
        out_specs=pl.BlockSpec((t, LANES), lambda p, b, i: (b * nb + jnp.maximum(i - 1, 0), p)),
        out_shape=jax.ShapeDtypeStruct((batch * s, npair * LANES), BF16),
        scratch_shapes=[pltpu.VMEM((2, s, t), F32), pltpu.VMEM((2, s, t), F32),
                        pltpu.VMEM((2, 1, t), F32), pltpu.VMEM((2, 1, t), F32)],
        compiler_params=_params("parallel", "parallel", "arbitrary"),
        name="moba_attn",
    )(qa, ka, va, bias)


HEAD_GROUP = 8


def _sample_scores_kernel(pt_ref, qt_ref, *refs, scale, pages_per_block):
    page_refs, o_ref = refs[:-1], refs[-1]
    nh, _, ps = page_refs[0].shape
    for h in range(nh):
        qc = qt_ref[:, h:h + 1] * scale
        for i, page in enumerate(page_refs):
            blk, p = divmod(i, pages_per_block)
            o_ref[blk, h:h + 1, p * ps:(p + 1) * ps] = jnp.sum(qc * page[h], axis=0, keepdims=True)


def sample_scores(qt, cache_t, layer, page_table, nbc):
    bsz = page_table.shape[0]
    _, _, nh, dh, ps = cache_t.shape
    ppb = MOBA_BLOCK // ps
    assert MOBA_BLOCK == ppb * ps
    bps = math.gcd(nbc, 4)
    page = lambda i: pl.BlockSpec((None, None, nh, dh, ps),
                                  lambda b, j, pt: (layer, pt[b, j * bps * ppb + i], 0, 0, 0))
    return pl.pallas_call(
        functools.partial(_sample_scores_kernel, scale=dh ** -0.5, pages_per_block=ppb),
        grid_spec=pltpu.PrefetchScalarGridSpec(
            num_scalar_prefetch=1,
            grid=(bsz, nbc // bps),
            in_specs=[pl.BlockSpec((None, dh, nh), lambda b, j, pt: (b, 0, 0))]
                     + [page(i) for i in range(bps * ppb)],
            out_specs=pl.BlockSpec((None, bps, nh, MOBA_BLOCK), lambda b, j, pt: (b, j, 0, 0))),
        out_shape=jax.ShapeDtypeStruct((bsz, nbc, nh, MOBA_BLOCK), F32),
        compiler_params=_params("parallel", "arbitrary"),
        name="sample_scores",
    )(page_table, qt, *([cache_t] * (bps * ppb)))


def _sample_select_kernel(s_ref, sel_ref, *, k_sel, inv):
    nb, nh, _ = s_ref.shape
    gate = jnp.sum(s_ref[...], axis=-1) * inv
    n_avail = jnp.full((1, nh), nb, jnp.int32)
    _, picks = _top_k_rows(gate, n_avail, k_sel)
    for r, idx in enumerate(picks):
        sel_ref[r:r + 1, :] = idx.astype(jnp.int32)


def sample_select(scores, k_sel, scale):
    bsz, nb, nh, t = scores.shape
    return pl.pallas_call(
        functools.partial(_sample_select_kernel, k_sel=k_sel, inv=1.0 / (t * scale)),
        grid=(bsz,),
        in_specs=[pl.BlockSpec((None, nb, nh, t), lambda b: (b, 0, 0, 0))],
        out_specs=pl.BlockSpec((None, k_sel, nh), lambda b: (b, 0, 0)),
        out_shape=jax.ShapeDtypeStruct((bsz, k_sel, nh), jnp.int32),
        compiler_params=_params("parallel"),
        name="sample_select",
    )(scores)


def _sample_attn_kernel(pg_ref, sel_ref, rel_ref, q_ref, kn_ref, vn_ref, *refs, nh, k_sel, past):
    ppb = (len(refs) - 1 - k_sel) // k_sel
    s_refs, v_refs, o_ref = refs[:k_sel], refs[k_sel:k_sel + ppb * k_sel], refs[-1]
    b, h = pl.program_id(0), pl.program_id(1)
    dh, ps = v_refs[0].shape
    t = s_refs[0].shape[1]
    hl = h % q_ref.shape[0]
    q = q_ref[pl.ds(hl, 1), :] * (dh ** -0.5)
    lane = lax.broadcasted_iota(jnp.int32, (1, t), 1)
    scores = []
    for r in range(k_sel):
        blk = sel_ref[(b * nh + h) * k_sel + r]
        dist = past - (blk * MOBA_BLOCK + lane)
        scores.append(s_refs[r][pl.ds(h, 1), :] + _bias_lookup(_rel_bucket(dist), rel_ref, h))
    s_self = jnp.sum(q * kn_ref[pl.ds(hl, 1), :], axis=1, keepdims=True) + rel_ref[0, h]
    mx = s_self
    for s in scores:
        mx = jnp.maximum(mx, jnp.max(s, axis=1, keepdims=True))
    p_self = jnp.exp(s_self - mx)
    den = p_self
    acc = p_self * vn_ref[pl.ds(hl, 1), :]
    for r, s in enumerate(scores):
        p = jnp.exp(s - mx)
        den = den + jnp.sum(p, axis=1, keepdims=True)
        for i in range(ppb):
            p8 = jnp.broadcast_to(p[:, i * ps:(i + 1) * ps], (8, ps))
            acc = acc + _dot_f32(p8, v_refs[r * ppb + i][...], _dot_nt)[0:1, :]
    o_ref[...] = acc / den


def sample_attn(pages, sel_flat, rel_bias, q, k_new, v_new, scores, cache_vt, layer, k_sel, past):
    bsz, nh, dh = q.shape
    ps = cache_vt.shape[4]
    t = scores.shape[3]
    ppb = t // ps
    hg = HEAD_GROUP
    npg = ppb * k_sel

    def page(i):
        return pl.BlockSpec((None, None, None, dh, ps),
                            lambda b, h, pg, sel: (layer, pg[(b * nh + h) * npg + i], h, 0, 0))

    def picked(r):
        return pl.BlockSpec((None, None, nh, t), lambda b, h, pg, sel: (b, sel[(b * nh + h) * k_sel + r], 0, 0))

    new = lambda: pl.BlockSpec((None, hg, dh), lambda b, h, pg, sel: (b, h // hg, 0))
    return pl.pallas_call(
        functools.partial(_sample_attn_kernel, nh=nh, k_sel=k_sel, past=past),
        grid_spec=pltpu.PrefetchScalarGridSpec(
            num_scalar_prefetch=2,
            grid=(bsz, nh),
            in_specs=[pl.BlockSpec(memory_space=pltpu.SMEM), new(), new(), new()]
                     + [picked(r) for r in range(k_sel)] + [page(i) for i in range(npg)],
            out_specs=pl.BlockSpec((None, None, 1, dh), lambda b, h, pg, sel: (b, h, 0, 0))),
        out_shape=jax.ShapeDtypeStruct((bsz, nh, 1, dh), F32),
        compiler_params=_params("parallel", "arbitrary"),
        name="sample_attn",
    )(pages, sel_flat, rel_bias, q, k_new, v_new, *([scores] * k_sel), *([cache_vt] * npg))


def _mlstm_layer(hp, hs, batch, st_c, st_n, st_m, g_mix, w_in, b_gates, g_head, w_out):
    nh = ML_HEADS
    d = hp.shape[1]
    vw = w_out.shape[0]
    dv = vw // nh
    dqk = dv // 2
    qkw = nh * dqk
    w_main = jnp.concatenate([w_in[:, :qkw], w_in[:, 2 * qkw:2 * qkw + 2 * vw]], axis=1).astype(BF16)
    wkt = w_in[:, qkw:2 * qkw].T.astype(BF16)
    wgt = w_in[:, 2 * qkw + 2 * vw:].T
    w_out_b = w_out.astype(BF16)

    q, v, o, kt, gt = ml_proj(hp, g_mix, w_main, wkt, wgt, b_gates)
    hsp, c_p, n_p, m_p = mlstm_prompt(q, v, o, kt, gt, g_head, batch)
    hp = out_proj(hsp, w_out_b, hp)

    bs = hs.shape[0]
    q, v, o, kt, gt = ml_proj(hs, g_mix, w_main, wkt, wgt, b_gates)
    hss, c_s, n_s, m_s = mlstm_sample(q.reshape(bs, nh, dqk), kt.T.reshape(bs, nh, dqk), v.reshape(bs, nh, dv),
                                      o.reshape(bs, nh, dv), gt.T.reshape(bs, 1, 2 * nh), g_head,
                                      st_c, st_n, st_m.reshape(bs, 1, nh))
    hs = out_proj(hss.reshape(bs, vw), w_out_b, hs)
    return hp, hs, (c_p, n_p, m_p), (c_s, n_s, m_s.reshape(bs, nh))


def _moba_layer(hp, hs, batch, cache_kt, cache_vt, layer, page_table, bias, kv_group, rel_bias, g_mix, w_qkv,
                w_out, kbuf, vbuf):
    _, _, nh, dh, ps = cache_kt.shape
    w = w_qkv.shape[1] // 3
    assert dh == LANES // 2 and w == nh * dh
    w_qkv_b = w_qkv.astype(BF16)
    w_out_b = w_out.astype(BF16)

    q, k, v = norm_proj(hp, g_mix, w_qkv_b, (w, w, w))
    kmeans = block_means(k)
    qa, ka, va, kbuf, vbuf = moba_select(q, k, v, kmeans, batch, layer, kbuf, vbuf, kv_group)
    o = moba_attn(qa, ka, va, bias[0], kv_group, bias[1], bias[2])
    hp = out_proj(o, w_out_b, hp)

    bs, n_pages = page_table.shape
    past = n_pages * ps
    nbc = past // MOBA_BLOCK
    assert past == nbc * MOBA_BLOCK and nbc >= 1, "a partially filled tail block is not supported"
    k_sel = min(MOBA_TOPK, nbc)
    (qkv_s,) = norm_proj(hs, g_mix, w_qkv_b, (3 * w,))
    q_s, k_s, v_s = (qkv_s[:, i * w:(i + 1) * w].reshape(bs, nh, dh) for i in range(3))
    ppb = MOBA_BLOCK // ps
    scores = sample_scores(jnp.transpose(q_s, (0, 2, 1)), cache_kt, layer, page_table, nbc)
    sel = sample_select(scores, k_sel, dh ** -0.5)
    sel_bhk = jnp.transpose(sel, (0, 2, 1))
    logical = sel_bhk[..., None] * ppb + jnp.arange(ppb, dtype=jnp.int32)
    pages = jnp.take_along_axis(page_table, logical.reshape(bs, -1), axis=1)
    o_s = sample_attn(pages.reshape(-1), sel_bhk.reshape(-1), rel_bias, q_s, k_s, v_s,
                      scores, cache_vt, layer, k_sel, past)
    hs = out_proj(o_s.reshape(bs, w), w_out_b, hs)
    return hp, hs, kbuf, vbuf, (k_s, v_s)


def kernel(x_prompt, x_sample, cache_k, cache_v, state_C, state_n, state_m, page_table, norm_mix, norm_ffn,
           norm_final, rel_bias, w_ml_in, b_ml_gates, g_ml_head, w_ml_out, w_attn_qkv, w_attn_out, w_ffn_in,
           w_ffn_out):
    batch, seq, d = x_prompt.shape
    bs, dec_seq, _ = x_sample.shape
    assert dec_seq == 1 and seq % MOBA_BLOCK == 0
    depth = norm_mix.shape[0]
    nh_at = cache_k.shape[3]
    dh = cache_k.shape[4]
    hp = x_prompt.reshape(batch * seq, d)
    hs = x_sample.reshape(bs, d)
    w_ffn_in_b = w_ffn_in.astype(BF16)
    w_ffn_out_b = w_ffn_out.astype(BF16)
    nb = seq // MOBA_BLOCK
    kv_group = math.gcd(nb, KV_GROUP)
    d_min, d_sat = 1 - kv_group, _saturation_block_distance(MOBA_BLOCK, nb)
    bias = (bias_tiles(rel_bias, d_min, d_sat, MOBA_BLOCK), d_min, d_sat)
    cache_kt = jnp.transpose(cache_k, (0, 1, 3, 4, 2))
    cache_vt = jnp.transpose(cache_v, (0, 1, 3, 4, 2))

    kbuf = jnp.zeros((depth // 2, batch, nh_at * dh, seq), F32)
    vbuf = jnp.zeros((depth // 2, batch, nh_at * dh, seq), F32)
    kv_s, st_p, st_s = [], [], []
    for layer in range(depth):
        j = layer // 2
        if layer % 2 == 0:
            hp, hs, sp, ss = _mlstm_layer(hp, hs, batch, state_C[j], state_n[j], state_m[j], norm_mix[layer],
                                          w_ml_in[j], b_ml_gates[j], g_ml_head[j], w_ml_out[j])
            st_p.append(sp)
            st_s.append(ss)
        else:
            hp, hs, kbuf, vbuf, ks = _moba_layer(hp, hs, batch, cache_kt, cache_vt, j, page_table, bias, kv_group,
                                                 rel_bias, norm_mix[layer], w_attn_qkv[j], w_attn_out[j],
                                                 kbuf, vbuf)
            kv_s.append(ks)
        hp = ffn(hp, norm_ffn[layer], w_ffn_in_b[layer], w_ffn_out_b[layer])
        hs = ffn(hs, norm_ffn[layer], w_ffn_in_b[layer], w_ffn_out_b[layer])

    y_prompt = rmsnorm(hp, norm_final).reshape(batch, seq, d)
    y_sample = rmsnorm(hs, norm_final).reshape(bs, dec_seq, d)
    stack = lambda items, i, shape: jnp.stack([it[i] for it in items]).reshape((len(items),) + shape)
    to_rows = lambda buf: jnp.transpose(buf.reshape(buf.shape[0], batch, nh_at, dh, seq), (0, 1, 4, 2, 3))
    return (y_prompt, y_sample, to_rows(kbuf), to_rows(vbuf),
            stack(kv_s, 0, (bs, dec_seq, nh_at, dh)), stack(kv_s, 1, (bs, dec_seq, nh_at, dh)),
            jnp.stack([s[0] for s in st_p]), jnp.stack([s[1] for s in st_p]), jnp.stack([s[2] for s in st_p]),
            jnp.stack([s[0] for s in st_s]), jnp.stack([s[1] for s in st_s]), jnp.stack([s[2] for s in st_s]))
```

```python
import functools
import math

import jax
import jax.numpy as jnp
from jax import lax
from jax.experimental import pallas as pl
from jax.experimental.pallas import tpu as pltpu

F32 = jnp.float32
BF16 = jnp.bfloat16

EPS = 1e-6
ML_HEADS = 8
ML_CHUNK = 256
MOBA_BLOCK = 256
MOBA_TOPK = 3
KV_GROUP = 4
SUM_ROWS = 16
LOG2E = 1.4426950408889634
NUM_BUCKETS = 32
MAX_DISTANCE = 4096
NEG_BIG = -1e30
LANES = 128
VMEM_LIMIT = 56 * 1024 * 1024


def _dot(a, b):
    return jnp.dot(a, b, preferred_element_type=F32)


def _dot_nt(a, b):
    return lax.dot_general(a, b, (((1,), (1,)), ((), ())), preferred_element_type=F32)


def _split2(x):
    hi = x.astype(BF16)
    lo = (x - hi.astype(F32)).astype(BF16)
    return hi, lo


def _split3(x):
    hi = x.astype(BF16)
    r = x - hi.astype(F32)
    mid = r.astype(BF16)
    lo = (r - mid.astype(F32)).astype(BF16)
    return hi, mid, lo


def _dot_f32(a, b, dot=_dot):
    ah, al = _split2(a)
    bh, bl = _split2(b)
    return dot(ah, bh) + dot(ah, bl) + dot(al, bh)


def _norm_rows(x, g):
    return x * lax.rsqrt(jnp.mean(x * x, axis=-1, keepdims=True) + EPS) * g


def _log_sigmoid(x):
    return jnp.minimum(x, 0.0) - jnp.log1p(jnp.exp(-jnp.abs(x)))


def _rel_bucket(dist):
    n = jnp.maximum(dist, 0)
    max_exact = NUM_BUCKETS // 2
    nf = jnp.maximum(n, 1).astype(F32)
    large = max_exact + (jnp.log(nf / max_exact) / math.log(MAX_DISTANCE / max_exact)
                         * (NUM_BUCKETS - max_exact)).astype(jnp.int32)
    return jnp.where(n < max_exact, n, jnp.minimum(large, NUM_BUCKETS - 1))


def _bias_lookup(bucket, rel_ref, h):
    acc = jnp.full(bucket.shape, rel_ref[0, h], F32)
    for k in range(1, NUM_BUCKETS):
        acc = jnp.where(bucket == k, rel_ref[k, h], acc)
    return acc


def _params(*sem):
    return pltpu.CompilerParams(dimension_semantics=sem, vmem_limit_bytes=VMEM_LIMIT)


def _row_tile(m, target):
    t = min(m, target)
    assert m % t == 0, (m, t)
    return t


def _proj_kernel(x_ref, g_ref, w_ref, *o_refs):
    xb = _norm_rows(x_ref[...], g_ref[...]).astype(BF16)
    c0 = 0
    for o_ref in o_refs:
        n = o_ref.shape[1]
        for c in range(0, n, 512):
            w = min(512, n - c)
            o_ref[:, c:c + w] = _dot(xb, w_ref[:, c0 + c:c0 + c + w])
        c0 += n


def norm_proj(x, g, w, widths, tm=512):
    m, d = x.shape
    tm = _row_tile(m, tm)
    n = w.shape[1]
    assert sum(widths) == n
    return pl.pallas_call(
        _proj_kernel,
        grid=(m // tm,),
        in_specs=[pl.BlockSpec((tm, d), lambda i: (i, 0)),
                  pl.BlockSpec((1, d), lambda i: (0, 0)),
                  pl.BlockSpec((d, n), lambda i: (0, 0))],
        out_specs=[pl.BlockSpec((tm, wd), lambda i: (i, 0)) for wd in widths],
        out_shape=[jax.ShapeDtypeStruct((m, wd), F32) for wd in widths],
        compiler_params=_params("parallel"),
        name="norm_proj",
    )(x, g.reshape(1, d), w)


def _ml_proj_kernel(x_ref, g_ref, w_ref, wkt_ref, wgt_ref, bg_ref, q_ref, v_ref, o_ref, kt_ref, gt_ref):
    xn = _norm_rows(x_ref[...], g_ref[...])
    xb = xn.astype(BF16)
    xlo = (xn - xb.astype(F32)).astype(BF16)
    c0 = 0
    for out in (q_ref, v_ref, o_ref):
        n = out.shape[1]
        for c in range(0, n, 512):
            out[:, c:c + 512] = _dot(xb, w_ref[:, c0 + c:c0 + c + 512])
        c0 += n
    kt_ref[...] = _dot_nt(wkt_ref[...], xb)
    gh, gl = _split2(wgt_ref[...])
    gt_ref[...] = _dot_nt(gh, xb) + _dot_nt(gh, xlo) + _dot_nt(gl, xb) + bg_ref[...]


def ml_proj(x, g, w_main, wkt, wgt, bg, tm=512):
    m, d = x.shape
    tm = _row_tile(m, tm)
    qk = wkt.shape[0]
    vw = (w_main.shape[1] - qk) // 2
    ng = wgt.shape[0]
    return pl.pallas_call(
        _ml_proj_kernel,
        grid=(m // tm,),
        in_specs=[pl.BlockSpec((tm, d), lambda i: (i, 0)),
                  pl.BlockSpec((1, d), lambda i: (0, 0)),
                  pl.BlockSpec(w_main.shape, lambda i: (0, 0)),
                  pl.BlockSpec(wkt.shape, lambda i: (0, 0)),
                  pl.BlockSpec(wgt.shape, lambda i: (0, 0)),
                  pl.BlockSpec((ng, 1), lambda i: (0, 0))],
        out_specs=[pl.BlockSpec((tm, qk), lambda i: (i, 0)),
                   pl.BlockSpec((tm, vw), lambda i: (i, 0)),
                   pl.BlockSpec((tm, vw), lambda i: (i, 0)),
                   pl.BlockSpec((qk, tm), lambda i: (0, i)),
                   pl.BlockSpec((ng, tm), lambda i: (0, i))],
        out_shape=[jax.ShapeDtypeStruct((m, qk), F32),
                   jax.ShapeDtypeStruct((m, vw), F32),
                   jax.ShapeDtypeStruct((m, vw), F32),
                   jax.ShapeDtypeStruct((qk, m), F32),
                   jax.ShapeDtypeStruct((ng, m), F32)],
        compiler_params=_params("parallel"),
        name="ml_proj",
    )(x, g.reshape(1, d), w_main, wkt, wgt, bg.reshape(ng, 1))


def _proj_ffn_kernel(a_ref, wm_ref, r_ref, g_ref, wg_ref, wu_ref, wo_ref, *rest):
    gf_ref = rest[0] if len(rest) == 4 else None
    o_ref, xn_s, acc_s = rest[-3:]
    f = pl.program_id(1)

    @pl.when(f == 0)
    def _():
        h = r_ref[...] + _dot(a_ref[...].astype(BF16), wm_ref[...])
        xn_s[...] = _norm_rows(h, g_ref[...]).astype(BF16)
        acc_s[...] = h

    xb = xn_s[...]
    gp = _dot(xb, wg_ref[...])
    up = _dot(xb, wu_ref[...])
    act = (gp * jax.nn.sigmoid(gp) * up).astype(BF16)
    acc_s[...] += _dot(act, wo_ref[...])

    @pl.when(f == pl.num_programs(1) - 1)
    def _():
        out = acc_s[...]
        o_ref[...] = out if gf_ref is None else _norm_rows(out, gf_ref[...])


def proj_ffn(a, w_mix, res, g, w_in, w_out, g_final=None, tm=1024, tf=256):
    m, k = a.shape
    d = res.shape[1]
    dff = w_out.shape[0]
    tm = _row_tile(m, tm)
    nf = dff // tf
    assert dff % tf == 0
    row = lambda: pl.BlockSpec((1, d), lambda i, f: (0, 0))
    in_specs = [pl.BlockSpec((tm, k), lambda i, f: (i, 0)),
                pl.BlockSpec((k, d), lambda i, f: (0, 0)),
                pl.BlockSpec((tm, d), lambda i, f: (i, 0)),
                row(),
                pl.BlockSpec((d, tf), lambda i, f: (0, f)),
                pl.BlockSpec((d, tf), lambda i, f: (0, nf + f)),
                pl.BlockSpec((tf, d), lambda i, f: (f, 0))]
    args = [a, w_mix, res, g.reshape(1, d), w_in, w_in, w_out]
    if g_final is not None:
        in_specs.append(row())
        args.append(g_final.reshape(1, d))
    return pl.pallas_call(
        _proj_ffn_kernel,
        grid=(m // tm, nf),
        in_specs=in_specs,
        out_specs=pl.BlockSpec((tm, d), lambda i, f: (i, 0)),
        out_shape=jax.ShapeDtypeStruct((m, d), F32),
        scratch_shapes=[pltpu.VMEM((tm, d), BF16), pltpu.VMEM((tm, d), F32)],
        compiler_params=_params("parallel", "arbitrary"),
        name="proj_ffn",
    )(*args)


def _mlstm_chunk_kernel(q_ref, v_ref, o_ref, kt_ref, gt_ref, gh_ref, hs_ref, caug_ref, m_ref, caug_s, m_s):
    c = pl.program_id(1)
    L = q_ref.shape[0]
    nh = caug_s.shape[0]
    dqk = caug_s.shape[1]
    dv = caug_s.shape[2] // 2

    @pl.when(c == 0)
    def _():
        caug_s[...] = jnp.zeros_like(caug_s)
        m_s[...] = jnp.zeros_like(m_s)

    row = lax.broadcasted_iota(jnp.int32, (L, L), 0)
    col = lax.broadcasted_iota(jnp.int32, (L, L), 1)
    causal = col <= row
    lower = jnp.where(causal, 1.0, 0.0).astype(BF16)
    upper = jnp.where(row <= col, 1.0, 0.0).astype(BF16)
    gt = gt_ref[...]
    ig = gt[0:nh, :]
    logf = _log_sigmoid(gt[nh:2 * nh, :])
    parts = _split3(logf)
    f_row = _dot(parts[0], upper) + _dot(parts[1], upper) + _dot(parts[2], upper)
    f_col = _dot_nt(lower, parts[0]) + _dot_nt(lower, parts[1]) + _dot_nt(lower, parts[2])
    a_rows = ig - f_row
    ones_col = jnp.where(lax.broadcasted_iota(jnp.int32, (L, dv), 1) == 0, 1.0, 0.0).astype(BF16)

    for h in range(nh):
        m_prev = m_s[h:h + 1, 0:1]
        a_row = a_rows[h:h + 1, :]
        amat = jnp.where(causal, a_row, -jnp.inf)
        g = jnp.maximum(jnp.max(amat, axis=1, keepdims=True), m_prev)
        w = jnp.exp(amat - g)
        qh = q_ref[:, h * dqk:(h + 1) * dqk].astype(BF16)
        kth = kt_ref[h * dqk:(h + 1) * dqk, :] * (dqk ** -0.5)
        p = (_dot(qh, kth.astype(BF16)) * w).astype(BF16)
        vaug = jnp.concatenate([v_ref[:, h * dv:(h + 1) * dv].astype(BF16), ones_col], axis=1)
        caug = caug_s[h]
        a_int = jnp.exp(m_prev - g)
        tot = a_int * _dot(qh, caug.astype(BF16)) + _dot(p, vaug)
        m_t = f_col[:, h:h + 1] + g
        hh = tot[:, :dv] / jnp.maximum(jnp.abs(tot[:, dv:dv + 1]), jnp.exp(-m_t))
        hn = hh * lax.rsqrt(jnp.mean(hh * hh, axis=-1, keepdims=True) + EPS) * gh_ref[:, h * dv:(h + 1) * dv]
        hs_ref[:, h * dv:(h + 1) * dv] = (hn * jax.nn.sigmoid(o_ref[:, h * dv:(h + 1) * dv])).astype(hs_ref.dtype)
        g_last = g[L - 1:L, :]
        w_last = jnp.exp(a_row - g_last)
        caug_s[h] = jnp.exp(m_prev - g_last) * caug + _dot((kth * w_last).astype(BF16), vaug)
        m_s[h:h + 1, :] = jnp.broadcast_to(f_row[h:h + 1, L - 1:L] + g_last, (1, m_s.shape[1]))

    @pl.when(c == pl.num_programs(1) - 1)
    def _():
        caug_ref[...] = caug_s[...]
        m_ref[...] = m_s[...]


def mlstm_prompt(q, v, o, kt, gt, g_head, batch):
    m, qkw = q.shape
    vw = v.shape[1]
    nh = gt.shape[0] // 2
    dqk, dv = qkw // nh, vw // nh
    s = m // batch
    L = math.gcd(s, ML_CHUNK)
    nc = s // L
    hs, caug, mm = pl.pallas_call(
        _mlstm_chunk_kernel,
        grid=(batch, nc),
        in_specs=[pl.BlockSpec((L, qkw), lambda b, c: (b * nc + c, 0)),
                  pl.BlockSpec((L, vw), lambda b, c: (b * nc + c, 0)),
                  pl.BlockSpec((L, vw), lambda b, c: (b * nc + c, 0)),
                  pl.BlockSpec((qkw, L), lambda b, c: (0, b * nc + c)),
                  pl.BlockSpec((2 * nh, L), lambda b, c: (0, b * nc + c)),
                  pl.BlockSpec((1, vw), lambda b, c: (0, 0))],
        out_specs=[pl.BlockSpec((L, vw), lambda b, c: (b * nc + c, 0)),
                   pl.BlockSpec((None, nh, dqk, 2 * dv), lambda b, c: (b, 0, 0, 0)),
                   pl.BlockSpec((None, nh, LANES), lambda b, c: (b, 0, 0))],
        out_shape=[jax.ShapeDtypeStruct((m, vw), BF16),
                   jax.ShapeDtypeStruct((batch, nh, dqk, 2 * dv), F32),
                   jax.ShapeDtypeStruct((batch, nh, LANES), F32)],
        scratch_shapes=[pltpu.VMEM((nh, dqk, 2 * dv), F32), pltpu.VMEM((nh, LANES), F32)],
        compiler_params=_params("parallel", "arbitrary"),
        name="mlstm_prompt",
    )(q, v, o, kt, gt, g_head.reshape(1, vw))
    return hs, caug[..., :dv], caug[..., dv], mm[..., 0]


def _mlstm_step_kernel(q_ref, k_ref, v_ref, o_ref, gt_ref, m_ref, gh_ref, c_ref, n_ref,
                       hs_ref, cn_ref, nn_ref, mn_ref):
    nh, dqk, dv = c_ref.shape
    scale = dqk ** -0.5
    q = q_ref[...]
    k = k_ref[...] * scale
    eye = jnp.where(lax.broadcasted_iota(jnp.int32, (dqk, dqk), 0)
                    == lax.broadcasted_iota(jnp.int32, (dqk, dqk), 1), 1.0, 0.0).astype(BF16)
    q3, k3 = _split3(q), _split3(k)
    q_cols = _dot_nt(eye, q3[0]) + _dot_nt(eye, q3[1]) + _dot_nt(eye, q3[2])
    k_cols = _dot_nt(eye, k3[0]) + _dot_nt(eye, k3[1]) + _dot_nt(eye, k3[2])
    gt = gt_ref[...]
    for h in range(nh):
        ig = gt[:, h:h + 1]
        logf = _log_sigmoid(gt[:, nh + h:nh + h + 1])
        m_inter = m_ref[:, h:h + 1] + logf
        m_t = jnp.maximum(m_inter, ig)
        w = jnp.exp(ig - m_t)
        a = jnp.exp(m_inter - m_t)
        q_row, k_row = q[h:h + 1, :], k[h:h + 1, :]
        q_col, k_col = q_cols[:, h:h + 1], k_cols[:, h:h + 1]
        v_row = v_ref[h:h + 1, :]
        c_h = c_ref[h]
        n_row = n_ref[h:h + 1, :]
        sqk = jnp.sum(q_row * k_row, axis=1, keepdims=True) * w
        num = a * jnp.sum(q_col * c_h, axis=0, keepdims=True) + sqk * v_row
        den = a * jnp.sum(q_row * n_row, axis=1, keepdims=True) + sqk
        hh = num / jnp.maximum(jnp.abs(den), jnp.exp(-m_t))
        hn = hh * lax.rsqrt(jnp.mean(hh * hh, axis=-1, keepdims=True) + EPS) * gh_ref[h:h + 1, :]
        hs_ref[h:h + 1, :] = (hn * jax.nn.sigmoid(o_ref[h:h + 1, :])).astype(hs_ref.dtype)
        cn_ref[h] = a * c_h + (w * k_col) * v_row
        nn_ref[h:h + 1, :] = a * n_row + w * k_row
        mn_ref[:, h:h + 1] = m_t


def mlstm_sample(q, k, v, o, gates, g_head, c0, n0, m0):
    bsz, nh, dqk, dv = c0.shape
    row3 = lambda w: pl.BlockSpec((None, nh, w), lambda b: (b, 0, 0))
    return pl.pallas_call(
        _mlstm_step_kernel,
        grid=(bsz,),
        in_specs=[row3(dqk), row3(dqk), row3(dv), row3(dv),
                  pl.BlockSpec((None, 1, 2 * nh), lambda b: (b, 0, 0)),
                  pl.BlockSpec((None, 1, nh), lambda b: (b, 0, 0)),
                  pl.BlockSpec((nh, dv), lambda b: (0, 0)),
                  pl.BlockSpec((None, nh, dqk, dv), lambda b: (b, 0, 0, 0)),
                  row3(dqk)],
        out_specs=[row3(dv),
                   pl.BlockSpec((None, nh, dqk, dv), lambda b: (b, 0, 0, 0)),
                   row3(dqk),
                   pl.BlockSpec((None, 1, nh), lambda b: (b, 0, 0))],
        out_shape=[jax.ShapeDtypeStruct((bsz, nh, dv), F32),
                   jax.ShapeDtypeStruct((bsz, nh, dqk, dv), F32),
                   jax.ShapeDtypeStruct((bsz, nh, dqk), F32),
                   jax.ShapeDtypeStruct((bsz, 1, nh), F32)],
        compiler_params=_params("parallel"),
        name="mlstm_sample",
    )(q, k, v, o, gates, m0, g_head.reshape(nh, dv), c0, n0)


def _bucket_bounds(lo, hi):
    def bucket(n):
        n = max(n, 0)
        max_exact = NUM_BUCKETS // 2
        if n < max_exact:
            return n
        return min(max_exact + int(math.log(n / max_exact) / math.log(MAX_DISTANCE / max_exact)
                                   * (NUM_BUCKETS - max_exact)), NUM_BUCKETS - 1)
    return max(bucket(lo) - 1, 0), min(bucket(hi) + 1, NUM_BUCKETS - 1)


def _bias_tiles_kernel(kr_ref, rel_ref, o_ref, *, d_min):
    step = pl.program_id(0)
    d = step + d_min
    nh, t, _ = o_ref.shape
    dist = d * t + lax.broadcasted_iota(jnp.int32, (t, t), 1) - lax.broadcasted_iota(jnp.int32, (t, t), 0)
    bucket = _rel_bucket(dist)
    k_lo, k_hi = kr_ref[2 * step], kr_ref[2 * step + 1]
    for h in range(nh):
        acc = lax.fori_loop(k_lo + 1, k_hi + 1, lambda k, a: jnp.where(bucket == k, rel_ref[k, h], a),
                            jnp.full((t, t), rel_ref[k_lo, h], F32))
        o_ref[h] = jnp.where(dist >= 0, acc * LOG2E, NEG_BIG)


def bias_tiles(rel_bias, d_min, d_max, t):
    nh = rel_bias.shape[1]
    nd = d_max - d_min + 1
    bounds = [b for d in range(d_min, d_max + 1) for b in _bucket_bounds(d * t - (t - 1), d * t + (t - 1))]
    return pl.pallas_call(
        functools.partial(_bias_tiles_kernel, d_min=d_min),
        grid_spec=pltpu.PrefetchScalarGridSpec(
            num_scalar_prefetch=1,
            grid=(nd,),
            in_specs=[pl.BlockSpec(memory_space=pltpu.SMEM)],
            out_specs=pl.BlockSpec((nh, None, t, t), lambda d, kr: (0, d, 0, 0))),
        out_shape=jax.ShapeDtypeStruct((nh, nd, t, t), F32),
        compiler_params=_params("parallel"),
        name="bias_tiles",
    )(jnp.asarray(bounds, jnp.int32), rel_bias)


def _kmeans_kernel(k_ref, o_ref):
    nblk = o_ref.shape[0]
    t = k_ref.shape[0] // nblk
    for j in range(nblk):
        o_ref[j:j + 1, :] = jnp.sum(k_ref[j * t:(j + 1) * t, :], axis=0, keepdims=True) * (1.0 / t)


def block_means(k, blocks_per_step=8):
    m, w = k.shape
    rows = MOBA_BLOCK * blocks_per_step
    assert m % rows == 0
    return pl.pallas_call(
        _kmeans_kernel,
        grid=(m // rows,),
        in_specs=[pl.BlockSpec((rows, w), lambda i: (i, 0))],
        out_specs=pl.BlockSpec((blocks_per_step, w), lambda i: (i, 0)),
        out_shape=jax.ShapeDtypeStruct((m // MOBA_BLOCK, w), F32),
        compiler_params=_params("parallel"),
        name="block_means",
    )(k)


def _top_k_rows(gate, n_avail, k_sel):
    nb = gate.shape[0]
    jidx = lax.broadcasted_iota(jnp.int32, gate.shape, 0).astype(F32)
    selected = jnp.zeros(gate.shape, F32)
    picks = []
    for r in range(k_sel):
        mx = jnp.max(gate, axis=0, keepdims=True)
        idx = jnp.min(jnp.where(gate == mx, jidx, float(nb)), axis=0, keepdims=True)
        hit = jidx == idx
        selected = jnp.where(jnp.logical_and(hit, r < n_avail), 1.0, selected)
        gate = jnp.where(hit, -jnp.inf, gate)
        picks.append(idx)
    return selected > 0.5, picks


def _moba_select_kernel(q_ref, k_ref, v_ref, km_ref, kbuf_ref, vbuf_ref, qa_ref, ka_ref, va_ref, kt_ref, vt_ref,
                        *, nb, k_sel):
    del kbuf_ref, vbuf_ref
    t = pl.program_id(2)
    tq = q_ref.shape[0]
    half = LANES // 2
    shift = MOBA_BLOCK.bit_length() - 1
    q2, k2 = q_ref[...], k_ref[...]
    qt = q2.T * (half ** -0.5 * LOG2E)
    vt = v_ref[...].T
    kt_ref[...] = k2.T
    vt_ref[...] = vt
    ngrp, _, gw = va_ref.shape
    va = jnp.concatenate([vt.astype(BF16), jnp.ones((SUM_ROWS, tq), BF16)], axis=0)
    for gi in range(ngrp):
        va_ref[gi] = va[:, gi * gw:(gi + 1) * gw]
    km = km_ref[...]
    lane = lax.broadcasted_iota(jnp.int32, (tq, LANES), 1)
    rowblk = (t * tq + lax.broadcasted_iota(jnp.int32, (tq, LANES), 0)) >> shift
    feat = lax.broadcasted_iota(jnp.int32, (LANES, tq), 0)
    km_lane = lax.broadcasted_iota(jnp.int32, km.shape, 1)
    pos = t * tq + lax.broadcasted_iota(jnp.int32, (1, tq), 1)
    n_avail = jnp.minimum(pos >> shift, nb)
    jidx = lax.broadcasted_iota(jnp.int32, (nb, tq), 0)
    for hh in range(2):
        off = 0 if hh else half
        kmh = jnp.where((km_lane >= half) if hh else (km_lane < half), km, 0.0)
        gate = _dot_f32(kmh, q2, _dot_nt)
        gate = jnp.where(jidx < n_avail, gate, -jnp.inf)
        selected, _ = _top_k_rows(gate, n_avail, k_sel)
        code = jnp.where(jnp.logical_or(selected, jidx == (pos >> shift)), 0.0, NEG_BIG)
        pieces = [code, jnp.zeros((LANES - nb - off, tq), F32)]
        if off:
            pieces = [jnp.zeros((off, tq), F32)] + pieces
        code_rows = jnp.concatenate(pieces, axis=0)
        qa_ref[hh] = jnp.where((feat >= half) if hh else (feat < half), qt, code_rows).astype(BF16)
        onehot = jnp.where((lane - off) == rowblk, 1.0, 0.0)
        ka_ref[hh] = jnp.where((lane >= half) if hh else (lane < half), k2, onehot).astype(BF16)


def moba_select(q, k, v, kmeans, batch, slot, kbuf, vbuf, kv_group, tq=2048):
    m, w = q.shape
    s = m // batch
    nb = s // MOBA_BLOCK
    npair = w // LANES
    gw = kv_group * MOBA_BLOCK
    tq = max(min(tq, s), gw)
    nt = s // tq
    assert nb <= LANES // 2 and s % tq == 0 and tq % gw == 0 and kbuf.shape[1:] == (batch, w, s)
    k_sel = min(MOBA_TOPK, nb)
    rows = lambda: pl.BlockSpec((tq, LANES), lambda b, p, t: (b * nt + t, p))
    feat = lambda: pl.BlockSpec((None, None, LANES, tq), lambda b, p, t: (slot, b, p, t))
    return pl.pallas_call(
        functools.partial(_moba_select_kernel, nb=nb, k_sel=k_sel),
        grid=(batch, npair, nt),
        in_specs=[rows(), rows(), rows(),
                  pl.BlockSpec((nb, LANES), lambda b, p, t: (b, p)),
                  pl.BlockSpec(memory_space=pl.ANY), pl.BlockSpec(memory_space=pl.ANY)],
        out_specs=[pl.BlockSpec((None, 2, LANES, tq), lambda b, p, t: (b, p, 0, t)),
                   pl.BlockSpec((None, 2, tq, LANES), lambda b, p, t: (b, p, t, 0)),
                   pl.BlockSpec((None, None, tq // gw, LANES + SUM_ROWS, gw), lambda b, p, t: (b, p, t, 0, 0)),
                   feat(), feat()],
        out_shape=[jax.ShapeDtypeStruct((batch, 2 * npair, LANES, s), BF16),
                   jax.ShapeDtypeStruct((batch, 2 * npair, s, LANES), BF16),
                   jax.ShapeDtypeStruct((batch, npair, s // gw, LANES + SUM_ROWS, gw), BF16),
                   jax.ShapeDtypeStruct(kbuf.shape, F32),
                   jax.ShapeDtypeStruct(vbuf.shape, F32)],
        input_output_aliases={4: 3, 5: 4},
        compiler_params=_params("parallel", "parallel", "parallel"),
        name="moba_select",
    )(q, k, v, kmeans, kbuf, vbuf)


def _moba_attn_kernel(qa_ref, ka_ref, va_ref, bias_ref, o_ref, s_a, s_b, m_a, m_b, *, kv_group, d_min, d_sat):
    s = pl.program_id(2)
    nb = pl.num_programs(2) - 1
    t = qa_ref.shape[2]
    gw = kv_group * t
    half = LANES // 2
    n_cur = jnp.where(s < nb, s // kv_group + 1, 0)
    n_prev = jnp.where(s > 0, (s + kv_group - 1) // kv_group, 0)
    n_both = jnp.minimum(n_cur, n_prev)
    qs = (qa_ref[0], qa_ref[1])

    def fold(x, op):
        return op(x.reshape(x.shape[0] // 8, 8, x.shape[1]), axis=0)

    def run(s_cur, m_cur, s_prev, m_prev):
        m_old = (m_prev[0], m_prev[1])

        def scores(g, rmax):
            start = pl.multiple_of(g * gw, gw)
            rmax = list(rmax)
            for hh in range(2):
                sc = _dot(ka_ref[hh, pl.ds(start, gw), :], qs[hh])
                for jj in range(kv_group):
                    d = jnp.minimum(s - (g * kv_group + jj), d_sat)
                    sj = sc[jj * t:(jj + 1) * t, :] + bias_ref[hh, d - d_min]
                    s_cur[hh, pl.ds(pl.multiple_of(start + jj * t, t), t), :] = sj
                    rmax[hh] = jnp.maximum(rmax[hh], fold(sj, jnp.max))
            return tuple(rmax)

        def values(g, accs):
            start = pl.multiple_of(g * gw, gw)
            vgrp = va_ref[g]
            out = []
            for hh in range(2):
                p = jnp.exp2((s_prev[hh, pl.ds(start, gw), :] - m_old[hh]).astype(BF16))
                out.append(accs[hh] + _dot(vgrp, p))
            return tuple(out)

        carry = (jnp.full((8, t), NEG_BIG, F32),) * 2 + (jnp.zeros((LANES + SUM_ROWS, t), F32),) * 2
        carry = lax.fori_loop(0, n_both, lambda g, c: scores(g, c[:2]) + values(g, c[2:]), carry)
        rmax = lax.fori_loop(n_both, n_cur, scores, carry[:2])
        acc0, acc1 = lax.fori_loop(n_both, n_prev, values, carry[2:])
        for hh in range(2):
            m_cur[hh] = jnp.max(rmax[hh], axis=0, keepdims=True)

        @pl.when(s > 0)
        def _():
            feat = lax.broadcasted_iota(jnp.int32, (LANES, t), 0)
            out_t = jnp.where(feat < half, acc0[:LANES] / acc0[LANES:LANES + 1],
                              acc1[:LANES] / acc1[LANES:LANES + 1])
            o_ref[...] = out_t.T.astype(o_ref.dtype)

    @pl.when(s % 2 == 0)
    def _():
        run(s_a, m_a, s_b, m_b)

    @pl.when(s % 2 == 1)
    def _():
        run(s_b, m_b, s_a, m_a)


def _saturation_block_distance(t, nb):
    max_exact = NUM_BUCKETS // 2
    n_sat = max_exact * (MAX_DISTANCE / max_exact) ** ((NUM_BUCKETS - 1 - max_exact) / (NUM_BUCKETS - max_exact))
    return min(math.ceil((1.02 * n_sat + t - 1) / t), nb - 1)


def moba_attn(qa, ka, va, bias, kv_group, d_min, d_sat):
    batch, nh, s, _ = ka.shape
    npair = nh // 2
    t = MOBA_BLOCK
    nb = s // t
    assert nb % kv_group == 0 and d_min <= 1 - kv_group and bias.shape[1] == d_sat - d_min + 1
    last = nb - 1
    return pl.pallas_call(
        functools.partial(_moba_attn_kernel, kv_group=kv_group, d_min=d_min, d_sat=d_sat),
        grid=(npair, batch, nb + 1),
        in_specs=[pl.BlockSpec((None, 2, LANES, t), lambda p, b, i: (b, p, 0, jnp.minimum(i, last))),
                  pl.BlockSpec((None, 2, s, LANES), lambda p, b, i: (b, p, 0, 0)),
                  pl.BlockSpec((None, None) + va.shape[2:], lambda p, b, i: (b, p, 0, 0, 0),
                               pipeline_mode=pl.Buffered(1)),
                  pl.BlockSpec((2,) + bias.shape[1:], lambda p, b, i: (p, 0, 0, 0),
                               pipeline_mode=pl.Buffered(1))],
        out_specs=pl.BlockSpec((t, LANES), lambda p, b, i: (b * nb + jnp.maximum(i - 1, 0), p)),
        out_shape=jax.ShapeDtypeStruct((batch * s, npair * LANES), BF16),
        scratch_shapes=[pltpu.VMEM((2, s, t), F32), pltpu.VMEM((2, s, t), F32),
                        pltpu.VMEM((2, 1, t), F32), pltpu.VMEM((2, 1, t), F32)],
        compiler_params=_params("parallel", "parallel", "arbitrary"),
        name="moba_attn",
    )(qa, ka, va, bias)


HEAD_GROUP = 8


def _sample_bias_kernel(relt_ref, o_ref, *, past):
    nh, n = o_ref.shape
    bucket = _rel_bucket(past - lax.broadcasted_iota(jnp.int32, (nh, n), 1))
    relt = relt_ref[...]
    acc = jnp.broadcast_to(relt[:, 0:1], (nh, n))
    for k in range(1, NUM_BUCKETS):
        acc = jnp.where(bucket == k, relt[:, k:k + 1], acc)
    o_ref[...] = acc


def sample_bias(rel_bias, past):
    nh = rel_bias.shape[1]
    return pl.pallas_call(
        functools.partial(_sample_bias_kernel, past=past),
        out_shape=jax.ShapeDtypeStruct((nh, past), F32),
        compiler_params=pltpu.CompilerParams(vmem_limit_bytes=VMEM_LIMIT),
        name="sample_bias",
    )(rel_bias.T)


def _sample_scores_kernel(pt_ref, qt_ref, bias_ref, *refs, scale, pages_per_block):
    page_refs, o_ref, g_ref = refs[:-2], refs[-2], refs[-1]
    nh, _, ps = page_refs[0].shape
    t = pages_per_block * ps
    for h in range(nh):
        qc = qt_ref[:, h:h + 1] * scale
        for blk in range(len(page_refs) // pages_per_block):
            tot = None
            for p in range(pages_per_block):
                sp = jnp.sum(qc * page_refs[blk * pages_per_block + p][h], axis=0, keepdims=True)
                lo = blk * t + p * ps
                o_ref[blk, h:h + 1, p * ps:(p + 1) * ps] = sp + bias_ref[h:h + 1, lo:lo + ps]
                tot = sp if tot is None else tot + sp
            g_ref[blk, h:h + 1, :] = tot


def sample_scores(qt, bias, cache_t, layer, page_table, nbc):
    bsz = page_table.shape[0]
    _, _, nh, dh, ps = cache_t.shape
    ppb = MOBA_BLOCK // ps
    assert MOBA_BLOCK == ppb * ps
    bps = math.gcd(nbc, 4)
    page = lambda i: pl.BlockSpec((None, None, nh, dh, ps),
                                  lambda b, j, pt: (layer, pt[b, j * bps * ppb + i], 0, 0, 0))
    return pl.pallas_call(
        functools.partial(_sample_scores_kernel, scale=dh ** -0.5, pages_per_block=ppb),
        grid_spec=pltpu.PrefetchScalarGridSpec(
            num_scalar_prefetch=1,
            grid=(bsz, nbc // bps),
            in_specs=[pl.BlockSpec((None, dh, nh), lambda b, j, pt: (b, 0, 0)),
                      pl.BlockSpec((nh, bps * MOBA_BLOCK), lambda b, j, pt: (0, j))]
                     + [page(i) for i in range(bps * ppb)],
            out_specs=[pl.BlockSpec((None, bps, nh, MOBA_BLOCK), lambda b, j, pt: (b, j, 0, 0)),
                       pl.BlockSpec((None, bps, nh, ps), lambda b, j, pt: (b, j, 0, 0))]),
        out_shape=[jax.ShapeDtypeStruct((bsz, nbc, nh, MOBA_BLOCK), F32),
                   jax.ShapeDtypeStruct((bsz, nbc, nh, ps), F32)],
        compiler_params=_params("parallel", "arbitrary"),
        name="sample_scores",
    )(page_table, qt, bias, *([cache_t] * (bps * ppb)))


def _sample_select_kernel(s_ref, sel_ref, *, k_sel, inv):
    nb, nh, _ = s_ref.shape
    gate = jnp.sum(s_ref[...], axis=-1) * inv
    n_avail = jnp.full((1, nh), nb, jnp.int32)
    _, picks = _top_k_rows(gate, n_avail, k_sel)
    for r, idx in enumerate(picks):
        sel_ref[r:r + 1, :] = idx.astype(jnp.int32)


def sample_select(gsum, k_sel, inv):
    bsz, nb, nh, t = gsum.shape
    scores = gsum
    return pl.pallas_call(
        functools.partial(_sample_select_kernel, k_sel=k_sel, inv=inv),
        grid=(bsz,),
        in_specs=[pl.BlockSpec((None, nb, nh, t), lambda b: (b, 0, 0, 0))],
        out_specs=pl.BlockSpec((None, k_sel, nh), lambda b: (b, 0, 0)),
        out_shape=jax.ShapeDtypeStruct((bsz, k_sel, nh), jnp.int32),
        compiler_params=_params("parallel"),
        name="sample_select",
    )(scores)


def _sample_attn_kernel(pg_ref, sel_ref, rel_ref, q_ref, kn_ref, vn_ref, *refs, nh, k_sel):
    s_ref, v_refs, o_ref = refs[0], refs[1:-1], refs[-1]
    ppb = len(v_refs) // k_sel
    b, h = pl.program_id(0), pl.program_id(1)
    dh, ps = v_refs[0].shape
    hl = h % q_ref.shape[0]
    q = q_ref[pl.ds(hl, 1), :] * (dh ** -0.5)
    scores = [s_ref[sel_ref[(b * nh + h) * k_sel + r], pl.ds(h, 1), :] for r in range(k_sel)]
    s_self = jnp.sum(q * kn_ref[pl.ds(hl, 1), :], axis=1, keepdims=True) + rel_ref[0, h]
    mx = s_self
    for s in scores:
        mx = jnp.maximum(mx, jnp.max(s, axis=1, keepdims=True))
    p_self = jnp.exp(s_self - mx)
    den = p_self
    acc = p_self * vn_ref[pl.ds(hl, 1), :]
    for r, s in enumerate(scores):
        p = jnp.exp(s - mx)
        den = den + jnp.sum(p, axis=1, keepdims=True)
        for i in range(ppb):
            p8 = jnp.broadcast_to(p[:, i * ps:(i + 1) * ps], (8, ps))
            acc = acc + _dot_f32(p8, v_refs[r * ppb + i][...], _dot_nt)[0:1, :]
    o_ref[...] = acc / den


def sample_attn(pages, sel_flat, rel_bias, q, k_new, v_new, scores, cache_vt, layer, k_sel):
    bsz, nh, dh = q.shape
    ps = cache_vt.shape[4]
    _, nbc, _, t = scores.shape
    ppb = t // ps
    hg = HEAD_GROUP
    npg = ppb * k_sel

    def page(i):
        return pl.BlockSpec((None, None, None, dh, ps),
                            lambda b, h, pg, sel: (layer, pg[(b * nh + h) * npg + i], h, 0, 0))

    new = lambda: pl.BlockSpec((None, hg, dh), lambda b, h, pg, sel: (b, h // hg, 0))
    return pl.pallas_call(
        functools.partial(_sample_attn_kernel, nh=nh, k_sel=k_sel),
        grid_spec=pltpu.PrefetchScalarGridSpec(
            num_scalar_prefetch=2,
            grid=(bsz, nh),
            in_specs=[pl.BlockSpec(memory_space=pltpu.SMEM), new(), new(), new(),
                      pl.BlockSpec((None, nbc, nh, t), lambda b, h, pg, sel: (b, 0, 0, 0))]
                     + [page(i) for i in range(npg)],
            out_specs=pl.BlockSpec((None, None, 1, dh), lambda b, h, pg, sel: (b, h, 0, 0))),
        out_shape=jax.ShapeDtypeStruct((bsz, nh, 1, dh), F32),
        compiler_params=_params("parallel", "arbitrary"),
        name="sample_attn",
    )(pages, sel_flat, rel_bias, q, k_new, v_new, scores, *([cache_vt] * npg))


def _mlstm_layer(hp, hs, batch, st_c, st_n, st_m, g_mix, w_in, b_gates, g_head, w_out):
    nh = ML_HEADS
    d = hp.shape[1]
    vw = w_out.shape[0]
    dv = vw // nh
    dqk = dv // 2
    qkw = nh * dqk
    w_main = jnp.concatenate([w_in[:, :qkw], w_in[:, 2 * qkw:2 * qkw + 2 * vw]], axis=1).astype(BF16)
    wkt = w_in[:, qkw:2 * qkw].T.astype(BF16)
    wgt = w_in[:, 2 * qkw + 2 * vw:].T
    w_out_b = w_out.astype(BF16)

    q, v, o, kt, gt = ml_proj(hp, g_mix, w_main, wkt, wgt, b_gates)
    hsp, c_p, n_p, m_p = mlstm_prompt(q, v, o, kt, gt, g_head, batch)

    bs = hs.shape[0]
    q, v, o, kt, gt = ml_proj(hs, g_mix, w_main, wkt, wgt, b_gates)
    hss, c_s, n_s, m_s = mlstm_sample(q.reshape(bs, nh, dqk), kt.T.reshape(bs, nh, dqk), v.reshape(bs, nh, dv),
                                      o.reshape(bs, nh, dv), gt.T.reshape(bs, 1, 2 * nh), g_head,
                                      st_c, st_n, st_m.reshape(bs, 1, nh))
    return hsp, hss.reshape(bs, vw), w_out_b, (c_p, n_p, m_p), (c_s, n_s, m_s.reshape(bs, nh))


def _moba_layer(hp, hs, batch, cache_kt, cache_vt, layer, page_table, bias, kv_group, rel_bias, g_mix, w_qkv,
                w_out, kbuf, vbuf):
    _, _, nh, dh, ps = cache_kt.shape
    w = w_qkv.shape[1] // 3
    assert dh == LANES // 2 and w == nh * dh
    w_qkv_b = w_qkv.astype(BF16)
    w_out_b = w_out.astype(BF16)

    q, k, v = norm_proj(hp, g_mix, w_qkv_b, (w, w, w))
    kmeans = block_means(k)
    qa, ka, va, kbuf, vbuf = moba_select(q, k, v, kmeans, batch, layer, kbuf, vbuf, kv_group)
    o = moba_attn(qa, ka, va, bias[0], kv_group, bias[1], bias[2])

    bs, n_pages = page_table.shape
    past = n_pages * ps
    sbias = bias[3]
    assert sbias.shape == (nh, past)
    nbc = past // MOBA_BLOCK
    assert past == nbc * MOBA_BLOCK and nbc >= 1, "a partially filled tail block is not supported"
    k_sel = min(MOBA_TOPK, nbc)
    (qkv_s,) = norm_proj(hs, g_mix, w_qkv_b, (3 * w,))
    q_s, k_s, v_s = (qkv_s[:, i * w:(i + 1) * w].reshape(bs, nh, dh) for i in range(3))
    ppb = MOBA_BLOCK // ps
    scores, gsum = sample_scores(jnp.transpose(q_s, (0, 2, 1)), sbias, cache_kt, layer, page_table, nbc)
    sel = sample_select(gsum, k_sel, 1.0 / (MOBA_BLOCK * dh ** -0.5))
    sel_bhk = jnp.transpose(sel, (0, 2, 1))
    logical = sel_bhk[..., None] * ppb + jnp.arange(ppb, dtype=jnp.int32)
    pages = jnp.take_along_axis(page_table, logical.reshape(bs, -1), axis=1)
    o_s = sample_attn(pages.reshape(-1), sel_bhk.reshape(-1), rel_bias, q_s, k_s, v_s,
                      scores, cache_vt, layer, k_sel)
    return o, o_s.reshape(bs, w), w_out_b, kbuf, vbuf, (k_s, v_s)


def kernel(x_prompt, x_sample, cache_k, cache_v, state_C, state_n, state_m, page_table, norm_mix, norm_ffn,
           norm_final, rel_bias, w_ml_in, b_ml_gates, g_ml_head, w_ml_out, w_attn_qkv, w_attn_out, w_ffn_in,
           w_ffn_out):
    batch, seq, d = x_prompt.shape
    bs, dec_seq, _ = x_sample.shape
    assert dec_seq == 1 and seq % MOBA_BLOCK == 0
    depth = norm_mix.shape[0]
    nh_at = cache_k.shape[3]
    dh = cache_k.shape[4]
    hp = x_prompt.reshape(batch * seq, d)
    hs = x_sample.reshape(bs, d)
    w_ffn_in_b = w_ffn_in.astype(BF16)
    w_ffn_out_b = w_ffn_out.astype(BF16)
    nb = seq // MOBA_BLOCK
    kv_group = math.gcd(nb, KV_GROUP)
    d_min, d_sat = 1 - kv_group, _saturation_block_distance(MOBA_BLOCK, nb)
    bias = (bias_tiles(rel_bias, d_min, d_sat, MOBA_BLOCK), d_min, d_sat,
            sample_bias(rel_bias, page_table.shape[1] * cache_k.shape[2]))
    cache_kt = jnp.transpose(cache_k, (0, 1, 3, 4, 2))
    cache_vt = jnp.transpose(cache_v, (0, 1, 3, 4, 2))

    kbuf = jnp.zeros((depth // 2, batch, nh_at * dh, seq), F32)
    vbuf = jnp.zeros((depth // 2, batch, nh_at * dh, seq), F32)
    kv_s, st_p, st_s = [], [], []
    for layer in range(depth):
        j = layer // 2
        if layer % 2 == 0:
            ap, asm, w_o, sp, ss = _mlstm_layer(hp, hs, batch, state_C[j], state_n[j], state_m[j], norm_mix[layer],
                                                w_ml_in[j], b_ml_gates[j], g_ml_head[j], w_ml_out[j])
            st_p.append(sp)
            st_s.append(ss)
        else:
            ap, asm, w_o, kbuf, vbuf, ks = _moba_layer(hp, hs, batch, cache_kt, cache_vt, j, page_table, bias,
                                                       kv_group, rel_bias, norm_mix[layer], w_attn_qkv[j],
                                                       w_attn_out[j], kbuf, vbuf)
            kv_s.append(ks)
        g_last = norm_final if layer == depth - 1 else None
        hp = proj_ffn(ap, w_o, hp, norm_ffn[layer], w_ffn_in_b[layer], w_ffn_out_b[layer], g_last)
        hs = proj_ffn(asm, w_o, hs, norm_ffn[layer], w_ffn_in_b[layer], w_ffn_out_b[layer], g_last)

    y_prompt = hp.reshape(batch, seq, d)
    y_sample = hs.reshape(bs, dec_seq, d)
    stack = lambda items, i, shape: jnp.stack([it[i] for it in items]).reshape((len(items),) + shape)
    to_rows = lambda buf: jnp.transpose(buf.reshape(buf.shape[0], batch, nh_at, dh, seq), (0, 1, 4, 2, 3))
    return (y_prompt, y_sample, to_rows(kbuf), to_rows(vbuf),
            stack(kv_s, 0, (bs, dec_seq, nh_at, dh)), stack(kv_s, 1, (bs, dec_seq, nh_at, dh)),
            jnp.stack([s[0] for s in st_p]), jnp.stack([s[1] for s in st_p]), jnp.stack([s[2] for s in st_p]),
            jnp.stack([s[0] for s in st_s]), jnp.stack([s[1] for s in st_s]), jnp.stack([s[2] for s in st_s]))
```

```python
import functools
import math

import jax
import jax.numpy as jnp
from jax import lax
from jax.experimental import pallas as pl
from jax.experimental.pallas import tpu as pltpu

F32 = jnp.float32
BF16 = jnp.bfloat16

EPS = 1e-6
ML_HEADS = 8
ML_CHUNK = 256
MOBA_BLOCK = 256
MOBA_TOPK = 3
KV_GROUP = 4
SUM_ROWS = 16
LOG2E = 1.4426950408889634
NUM_BUCKETS = 32
MAX_DISTANCE = 4096
NEG_BIG = -1e30
LANES = 128
VMEM_LIMIT = 56 * 1024 * 1024


def _dot(a, b):
    return jnp.dot(a, b, preferred_element_type=F32)


def _dot_nt(a, b):
    return lax.dot_general(a, b, (((1,), (1,)), ((), ())), preferred_element_type=F32)


def _split2(x):
    hi = x.astype(BF16)
    lo = (x - hi.astype(F32)).astype(BF16)
    return hi, lo


def _split3(x):
    hi = x.astype(BF16)
    r = x - hi.astype(F32)
    mid = r.astype(BF16)
    lo = (r - mid.astype(F32)).astype(BF16)
    return hi, mid, lo


def _dot_f32(a, b, dot=_dot):
    ah, al = _split2(a)
    bh, bl = _split2(b)
    return dot(ah, bh) + dot(ah, bl) + dot(al, bh)


def _norm_rows(x, g):
    return x * lax.rsqrt(jnp.mean(x * x, axis=-1, keepdims=True) + EPS) * g


def _log_sigmoid(x):
    return jnp.minimum(x, 0.0) - jnp.log1p(jnp.exp(-jnp.abs(x)))


def _rel_bucket(dist):
    n = jnp.maximum(dist, 0)
    max_exact = NUM_BUCKETS // 2
    nf = jnp.maximum(n, 1).astype(F32)
    large = max_exact + (jnp.log(nf / max_exact) / math.log(MAX_DISTANCE / max_exact)
                         * (NUM_BUCKETS - max_exact)).astype(jnp.int32)
    return jnp.where(n < max_exact, n, jnp.minimum(large, NUM_BUCKETS - 1))


def _bias_lookup(bucket, rel_ref, h):
    acc = jnp.full(bucket.shape, rel_ref[0, h], F32)
    for k in range(1, NUM_BUCKETS):
        acc = jnp.where(bucket == k, rel_ref[k, h], acc)
    return acc


def _params(*sem):
    return pltpu.CompilerParams(dimension_semantics=sem, vmem_limit_bytes=VMEM_LIMIT)


def _row_tile(m, target):
    t = min(m, target)
    assert m % t == 0, (m, t)
    return t


def _proj_kernel(x_ref, g_ref, w_ref, *o_refs):
    xb = _norm_rows(x_ref[...], g_ref[...]).astype(BF16)
    c0 = 0
    for o_ref in o_refs:
        n = o_ref.shape[1]
        for c in range(0, n, 512):
            w = min(512, n - c)
            o_ref[:, c:c + w] = _dot(xb, w_ref[:, c0 + c:c0 + c + w])
        c0 += n


def norm_proj(x, g, w, widths, tm=512):
    m, d = x.shape
    tm = _row_tile(m, tm)
    n = w.shape[1]
    assert sum(widths) == n
    return pl.pallas_call(
        _proj_kernel,
        grid=(m // tm,),
        in_specs=[pl.BlockSpec((tm, d), lambda i: (i, 0)),
                  pl.BlockSpec((1, d), lambda i: (0, 0)),
                  pl.BlockSpec((d, n), lambda i: (0, 0))],
        out_specs=[pl.BlockSpec((tm, wd), lambda i: (i, 0)) for wd in widths],
        out_shape=[jax.ShapeDtypeStruct((m, wd), F32) for wd in widths],
        compiler_params=_params("parallel"),
        name="norm_proj",
    )(x, g.reshape(1, d), w)


def _ml_proj_kernel(x_ref, g_ref, w_ref, wkt_ref, wgt_ref, bg_ref, q_ref, v_ref, o_ref, kt_ref, gt_ref):
    xn = _norm_rows(x_ref[...], g_ref[...])
    xb = xn.astype(BF16)
    xlo = (xn - xb.astype(F32)).astype(BF16)
    c0 = 0
    for out in (q_ref, v_ref, o_ref):
        n = out.shape[1]
        for c in range(0, n, 512):
            out[:, c:c + 512] = _dot(xb, w_ref[:, c0 + c:c0 + c + 512]).astype(out.dtype)
        c0 += n
    kt_ref[...] = _dot_nt(wkt_ref[...], xb)
    gh, gl = _split2(wgt_ref[...])
    gt_ref[...] = _dot_nt(gh, xb) + _dot_nt(gh, xlo) + _dot_nt(gl, xb) + bg_ref[...]


def ml_proj(x, g, w_main, wkt, wgt, bg, row_dtype, tm=512):
    m, d = x.shape
    tm = _row_tile(m, tm)
    qk = wkt.shape[0]
    vw = (w_main.shape[1] - qk) // 2
    ng = wgt.shape[0]
    return pl.pallas_call(
        _ml_proj_kernel,
        grid=(m // tm,),
        in_specs=[pl.BlockSpec((tm, d), lambda i: (i, 0)),
                  pl.BlockSpec((1, d), lambda i: (0, 0)),
                  pl.BlockSpec(w_main.shape, lambda i: (0, 0)),
                  pl.BlockSpec(wkt.shape, lambda i: (0, 0)),
                  pl.BlockSpec(wgt.shape, lambda i: (0, 0)),
                  pl.BlockSpec((ng, 1), lambda i: (0, 0))],
        out_specs=[pl.BlockSpec((tm, qk), lambda i: (i, 0)),
                   pl.BlockSpec((tm, vw), lambda i: (i, 0)),
                   pl.BlockSpec((tm, vw), lambda i: (i, 0)),
                   pl.BlockSpec((qk, tm), lambda i: (0, i)),
                   pl.BlockSpec((ng, tm), lambda i: (0, i))],
        out_shape=[jax.ShapeDtypeStruct((m, qk), row_dtype),
                   jax.ShapeDtypeStruct((m, vw), row_dtype),
                   jax.ShapeDtypeStruct((m, vw), row_dtype),
                   jax.ShapeDtypeStruct((qk, m), F32),
                   jax.ShapeDtypeStruct((ng, m), F32)],
        compiler_params=_params("parallel"),
        name="ml_proj",
    )(x, g.reshape(1, d), w_main, wkt, wgt, bg.reshape(ng, 1))


def _proj_ffn_kernel(a_ref, wm_ref, r_ref, g_ref, wg_ref, wu_ref, wo_ref, *rest):
    gf_ref = rest[0] if len(rest) == 4 else None
    o_ref, xn_s, acc_s = rest[-3:]
    f = pl.program_id(1)

    @pl.when(f == 0)
    def _():
        h = r_ref[...] + _dot(a_ref[...].astype(BF16), wm_ref[...])
        xn_s[...] = _norm_rows(h, g_ref[...]).astype(BF16)
        acc_s[...] = h

    xb = xn_s[...]
    gp = _dot(xb, wg_ref[...])
    up = _dot(xb, wu_ref[...])
    act = (gp * jax.nn.sigmoid(gp) * up).astype(BF16)
    acc_s[...] += _dot(act, wo_ref[...])

    @pl.when(f == pl.num_programs(1) - 1)
    def _():
        out = acc_s[...]
        o_ref[...] = out if gf_ref is None else _norm_rows(out, gf_ref[...])


def proj_ffn(a, w_mix, res, g, w_in, w_out, g_final=None, tm=1024, tf=256):
    m, k = a.shape
    d = res.shape[1]
    dff = w_out.shape[0]
    tm = _row_tile(m, tm)
    nf = dff // tf
    assert dff % tf == 0
    row = lambda: pl.BlockSpec((1, d), lambda i, f: (0, 0))
    in_specs = [pl.BlockSpec((tm, k), lambda i, f: (i, 0)),
                pl.BlockSpec((k, d), lambda i, f: (0, 0)),
                pl.BlockSpec((tm, d), lambda i, f: (i, 0)),
                row(),
                pl.BlockSpec((d, tf), lambda i, f: (0, f)),
                pl.BlockSpec((d, tf), lambda i, f: (0, nf + f)),
                pl.BlockSpec((tf, d), lambda i, f: (f, 0))]
    args = [a, w_mix, res, g.reshape(1, d), w_in, w_in, w_out]
    if g_final is not None:
        in_specs.append(row())
        args.append(g_final.reshape(1, d))
    return pl.pallas_call(
        _proj_ffn_kernel,
        grid=(m // tm, nf),
        in_specs=in_specs,
        out_specs=pl.BlockSpec((tm, d), lambda i, f: (i, 0)),
        out_shape=jax.ShapeDtypeStruct((m, d), F32),
        scratch_shapes=[pltpu.VMEM((tm, d), BF16), pltpu.VMEM((tm, d), F32)],
        compiler_params=_params("parallel", "arbitrary"),
        name="proj_ffn",
    )(*args)


def _mlstm_chunk_kernel(q_ref, v_ref, o_ref, kt_ref, gt_ref, gh_ref, hs_ref, caug_ref, m_ref, caug_s, m_s):
    c = pl.program_id(1)
    L = q_ref.shape[0]
    nh = caug_s.shape[0]
    dqk = caug_s.shape[1]
    dv = caug_s.shape[2] // 2

    @pl.when(c == 0)
    def _():
        caug_s[...] = jnp.zeros_like(caug_s)
        m_s[...] = jnp.zeros_like(m_s)

    row = lax.broadcasted_iota(jnp.int32, (L, L), 0)
    col = lax.broadcasted_iota(jnp.int32, (L, L), 1)
    causal = col <= row
    lower = jnp.where(causal, 1.0, 0.0).astype(BF16)
    upper = jnp.where(row <= col, 1.0, 0.0).astype(BF16)
    gt = gt_ref[...]
    ig = gt[0:nh, :]
    logf = _log_sigmoid(gt[nh:2 * nh, :])
    parts = _split3(logf)
    f_row = _dot(parts[0], upper) + _dot(parts[1], upper) + _dot(parts[2], upper)
    f_col = _dot_nt(lower, parts[0]) + _dot_nt(lower, parts[1]) + _dot_nt(lower, parts[2])
    a_rows = ig - f_row
    ones_col = jnp.where(lax.broadcasted_iota(jnp.int32, (L, dv), 1) == 0, 1.0, 0.0).astype(BF16)

    for h in range(nh):
        m_prev = m_s[h:h + 1, 0:1]
        a_row = a_rows[h:h + 1, :]
        amat = jnp.where(causal, a_row, -jnp.inf)
        g = jnp.maximum(jnp.max(amat, axis=1, keepdims=True), m_prev)
        w = jnp.exp(amat - g)
        qh = q_ref[:, h * dqk:(h + 1) * dqk].astype(BF16)
        kth = kt_ref[h * dqk:(h + 1) * dqk, :] * (dqk ** -0.5)
        p = (_dot(qh, kth.astype(BF16)) * w).astype(BF16)
        vaug = jnp.concatenate([v_ref[:, h * dv:(h + 1) * dv].astype(BF16), ones_col], axis=1)
        caug = caug_s[h]
        a_int = jnp.exp(m_prev - g)
        tot = a_int * _dot(qh, caug.astype(BF16)) + _dot(p, vaug)
        m_t = f_col[:, h:h + 1] + g
        hh = tot[:, :dv] / jnp.maximum(jnp.abs(tot[:, dv:dv + 1]), jnp.exp(-m_t))
        hn = hh * lax.rsqrt(jnp.mean(hh * hh, axis=-1, keepdims=True) + EPS) * gh_ref[:, h * dv:(h + 1) * dv]
        hs_ref[:, h * dv:(h + 1) * dv] = (hn * jax.nn.sigmoid(o_ref[:, h * dv:(h + 1) * dv].astype(F32))).astype(hs_ref.dtype)
        g_last = g[L - 1:L, :]
        w_last = jnp.exp(a_row - g_last)
        caug_s[h] = jnp.exp(m_prev - g_last) * caug + _dot((kth * w_last).astype(BF16), vaug)
        m_s[h:h + 1, :] = jnp.broadcast_to(f_row[h:h + 1, L - 1:L] + g_last, (1, m_s.shape[1]))

    @pl.when(c == pl.num_programs(1) - 1)
    def _():
        caug_ref[...] = caug_s[...]
        m_ref[...] = m_s[...]


def mlstm_prompt(q, v, o, kt, gt, g_head, batch):
    m, qkw = q.shape
    vw = v.shape[1]
    nh = gt.shape[0] // 2
    dqk, dv = qkw // nh, vw // nh
    s = m // batch
    L = math.gcd(s, ML_CHUNK)
    nc = s // L
    hs, caug, mm = pl.pallas_call(
        _mlstm_chunk_kernel,
        grid=(batch, nc),
        in_specs=[pl.BlockSpec((L, qkw), lambda b, c: (b * nc + c, 0)),
                  pl.BlockSpec((L, vw), lambda b, c: (b * nc + c, 0)),
                  pl.BlockSpec((L, vw), lambda b, c: (b * nc + c, 0)),
                  pl.BlockSpec((qkw, L), lambda b, c: (0, b * nc + c)),
                  pl.BlockSpec((2 * nh, L), lambda b, c: (0, b * nc + c)),
                  pl.BlockSpec((1, vw), lambda b, c: (0, 0))],
        out_specs=[pl.BlockSpec((L, vw), lambda b, c: (b * nc + c, 0)),
                   pl.BlockSpec((None, nh, dqk, 2 * dv), lambda b, c: (b, 0, 0, 0)),
                   pl.BlockSpec((None, nh, LANES), lambda b, c: (b, 0, 0))],
        out_shape=[jax.ShapeDtypeStruct((m, vw), BF16),
                   jax.ShapeDtypeStruct((batch, nh, dqk, 2 * dv), F32),
                   jax.ShapeDtypeStruct((batch, nh, LANES), F32)],
        scratch_shapes=[pltpu.VMEM((nh, dqk, 2 * dv), F32), pltpu.VMEM((nh, LANES), F32)],
        compiler_params=_params("parallel", "arbitrary"),
        name="mlstm_prompt",
    )(q, v, o, kt, gt, g_head.reshape(1, vw))
    return hs, caug[..., :dv], caug[..., dv], mm[..., 0]


def _mlstm_step_kernel(q_ref, k_ref, v_ref, o_ref, gt_ref, m_ref, gh_ref, c_ref, n_ref,
                       hs_ref, cn_ref, nn_ref, mn_ref):
    nh, dqk, dv = c_ref.shape
    scale = dqk ** -0.5
    q = q_ref[...]
    k = k_ref[...] * scale
    eye = jnp.where(lax.broadcasted_iota(jnp.int32, (dqk, dqk), 0)
                    == lax.broadcasted_iota(jnp.int32, (dqk, dqk), 1), 1.0, 0.0).astype(BF16)
    q3, k3 = _split3(q), _split3(k)
    q_cols = _dot_nt(eye, q3[0]) + _dot_nt(eye, q3[1]) + _dot_nt(eye, q3[2])
    k_cols = _dot_nt(eye, k3[0]) + _dot_nt(eye, k3[1]) + _dot_nt(eye, k3[2])
    gt = gt_ref[...]
    for h in range(nh):
        ig = gt[:, h:h + 1]
        logf = _log_sigmoid(gt[:, nh + h:nh + h + 1])
        m_inter = m_ref[:, h:h + 1] + logf
        m_t = jnp.maximum(m_inter, ig)
        w = jnp.exp(ig - m_t)
        a = jnp.exp(m_inter - m_t)
        q_row, k_row = q[h:h + 1, :], k[h:h + 1, :]
        q_col, k_col = q_cols[:, h:h + 1], k_cols[:, h:h + 1]
        v_row = v_ref[h:h + 1, :]
        c_h = c_ref[h]
        n_row = n_ref[h:h + 1, :]
        sqk = jnp.sum(q_row * k_row, axis=1, keepdims=True) * w
        num = a * jnp.sum(q_col * c_h, axis=0, keepdims=True) + sqk * v_row
        den = a * jnp.sum(q_row * n_row, axis=1, keepdims=True) + sqk
        hh = num / jnp.maximum(jnp.abs(den), jnp.exp(-m_t))
        hn = hh * lax.rsqrt(jnp.mean(hh * hh, axis=-1, keepdims=True) + EPS) * gh_ref[h:h + 1, :]
        hs_ref[h:h + 1, :] = (hn * jax.nn.sigmoid(o_ref[h:h + 1, :])).astype(hs_ref.dtype)
        cn_ref[h] = a * c_h + (w * k_col) * v_row
        nn_ref[h:h + 1, :] = a * n_row + w * k_row
        mn_ref[:, h:h + 1] = m_t


def mlstm_sample(q, k, v, o, gates, g_head, c0, n0, m0):
    bsz, nh, dqk, dv = c0.shape
    row3 = lambda w: pl.BlockSpec((None, nh, w), lambda b: (b, 0, 0))
    return pl.pallas_call(
        _mlstm_step_kernel,
        grid=(bsz,),
        in_specs=[row3(dqk), row3(dqk), row3(dv), row3(dv),
                  pl.BlockSpec((None, 1, 2 * nh), lambda b: (b, 0, 0)),
                  pl.BlockSpec((None, 1, nh), lambda b: (b, 0, 0)),
                  pl.BlockSpec((nh, dv), lambda b: (0, 0)),
                  pl.BlockSpec((None, nh, dqk, dv), lambda b: (b, 0, 0, 0)),
                  row3(dqk)],
        out_specs=[row3(dv),
                   pl.BlockSpec((None, nh, dqk, dv), lambda b: (b, 0, 0, 0)),
                   row3(dqk),
                   pl.BlockSpec((None, 1, nh), lambda b: (b, 0, 0))],
        out_shape=[jax.ShapeDtypeStruct((bsz, nh, dv), F32),
                   jax.ShapeDtypeStruct((bsz, nh, dqk, dv), F32),
                   jax.ShapeDtypeStruct((bsz, nh, dqk), F32),
                   jax.ShapeDtypeStruct((bsz, 1, nh), F32)],
        compiler_params=_params("parallel"),
        name="mlstm_sample",
    )(q, k, v, o, gates, m0, g_head.reshape(nh, dv), c0, n0)


def _bucket_bounds(lo, hi):
    def bucket(n):
        n = max(n, 0)
        max_exact = NUM_BUCKETS // 2
        if n < max_exact:
            return n
        return min(max_exact + int(math.log(n / max_exact) / math.log(MAX_DISTANCE / max_exact)
                                   * (NUM_BUCKETS - max_exact)), NUM_BUCKETS - 1)
    return max(bucket(lo) - 1, 0), min(bucket(hi) + 1, NUM_BUCKETS - 1)


def _bias_tiles_kernel(kr_ref, rel_ref, o_ref, *, d_min):
    step = pl.program_id(0)
    d = step + d_min
    nh, t, _ = o_ref.shape
    dist = d * t + lax.broadcasted_iota(jnp.int32, (t, t), 1) - lax.broadcasted_iota(jnp.int32, (t, t), 0)
    bucket = _rel_bucket(dist)
    k_lo, k_hi = kr_ref[2 * step], kr_ref[2 * step + 1]
    for h in range(nh):
        acc = lax.fori_loop(k_lo + 1, k_hi + 1, lambda k, a: jnp.where(bucket == k, rel_ref[k, h], a),
                            jnp.full((t, t), rel_ref[k_lo, h], F32))
        o_ref[h] = jnp.where(dist >= 0, acc * LOG2E, NEG_BIG)


def bias_tiles(rel_bias, d_min, d_max, t):
    nh = rel_bias.shape[1]
    nd = d_max - d_min + 1
    bounds = [b for d in range(d_min, d_max + 1) for b in _bucket_bounds(d * t - (t - 1), d * t + (t - 1))]
    return pl.pallas_call(
        functools.partial(_bias_tiles_kernel, d_min=d_min),
        grid_spec=pltpu.PrefetchScalarGridSpec(
            num_scalar_prefetch=1,
            grid=(nd,),
            in_specs=[pl.BlockSpec(memory_space=pltpu.SMEM)],
            out_specs=pl.BlockSpec((nh, None, t, t), lambda d, kr: (0, d, 0, 0))),
        out_shape=jax.ShapeDtypeStruct((nh, nd, t, t), F32),
        compiler_params=_params("parallel"),
        name="bias_tiles",
    )(jnp.asarray(bounds, jnp.int32), rel_bias)


def _kmeans_kernel(k_ref, o_ref):
    nblk = o_ref.shape[0]
    t = k_ref.shape[0] // nblk
    for j in range(nblk):
        o_ref[j:j + 1, :] = jnp.sum(k_ref[j * t:(j + 1) * t, :], axis=0, keepdims=True) * (1.0 / t)


def block_means(k, blocks_per_step=8):
    m, w = k.shape
    rows = MOBA_BLOCK * blocks_per_step
    assert m % rows == 0
    return pl.pallas_call(
        _kmeans_kernel,
        grid=(m // rows,),
        in_specs=[pl.BlockSpec((rows, w), lambda i: (i, 0))],
        out_specs=pl.BlockSpec((blocks_per_step, w), lambda i: (i, 0)),
        out_shape=jax.ShapeDtypeStruct((m // MOBA_BLOCK, w), F32),
        compiler_params=_params("parallel"),
        name="block_means",
    )(k)


def _top_k_rows(gate, n_avail, k_sel):
    nb = gate.shape[0]
    jidx = lax.broadcasted_iota(jnp.int32, gate.shape, 0).astype(F32)
    selected = jnp.zeros(gate.shape, F32)
    picks = []
    for r in range(k_sel):
        mx = jnp.max(gate, axis=0, keepdims=True)
        idx = jnp.min(jnp.where(gate == mx, jidx, float(nb)), axis=0, keepdims=True)
        hit = jidx == idx
        selected = jnp.where(jnp.logical_and(hit, r < n_avail), 1.0, selected)
        gate = jnp.where(hit, -jnp.inf, gate)
        picks.append(idx)
    return selected > 0.5, picks


def _moba_select_kernel(q_ref, k_ref, v_ref, km_ref, kbuf_ref, vbuf_ref, qa_ref, ka_ref, va_ref, kt_ref, vt_ref,
                        *, nb, k_sel):
    del kbuf_ref, vbuf_ref
    t = pl.program_id(2)
    tq = q_ref.shape[0]
    half = LANES // 2
    shift = MOBA_BLOCK.bit_length() - 1
    q2, k2 = q_ref[...], k_ref[...]
    qt = q2.T * (half ** -0.5 * LOG2E)
    vt = v_ref[...].T
    kt_ref[...] = k2.T
    vt_ref[...] = vt
    ngrp, _, gw = va_ref.shape
    va = jnp.concatenate([vt.astype(BF16), jnp.ones((SUM_ROWS, tq), BF16)], axis=0)
    for gi in range(ngrp):
        va_ref[gi] = va[:, gi * gw:(gi + 1) * gw]
    km = km_ref[...]
    lane = lax.broadcasted_iota(jnp.int32, (tq, LANES), 1)
    rowblk = (t * tq + lax.broadcasted_iota(jnp.int32, (tq, LANES), 0)) >> shift
    feat = lax.broadcasted_iota(jnp.int32, (LANES, tq), 0)
    km_lane = lax.broadcasted_iota(jnp.int32, km.shape, 1)
    pos = t * tq + lax.broadcasted_iota(jnp.int32, (1, tq), 1)
    n_avail = jnp.minimum(pos >> shift, nb)
    jidx = lax.broadcasted_iota(jnp.int32, (nb, tq), 0)
    for hh in range(2):
        off = 0 if hh else half
        kmh = jnp.where((km_lane >= half) if hh else (km_lane < half), km, 0.0)
        gate = _dot_f32(kmh, q2, _dot_nt)
        gate = jnp.where(jidx < n_avail, gate, -jnp.inf)
        selected, _ = _top_k_rows(gate, n_avail, k_sel)
        code = jnp.where(jnp.logical_or(selected, jidx == (pos >> shift)), 0.0, NEG_BIG)
        pieces = [code, jnp.zeros((LANES - nb - off, tq), F32)]
        if off:
            pieces = [jnp.zeros((off, tq), F32)] + pieces
        code_rows = jnp.concatenate(pieces, axis=0)
        qa_ref[hh] = jnp.where((feat >= half) if hh else (feat < half), qt, code_rows).astype(BF16)
        onehot = jnp.where((lane - off) == rowblk, 1.0, 0.0)
        ka_ref[hh] = jnp.where((lane >= half) if hh else (lane < half), k2, onehot).astype(BF16)


def moba_select(q, k, v, kmeans, batch, slot, kbuf, vbuf, kv_group, tq=2048):
    m, w = q.shape
    s = m // batch
    nb = s // MOBA_BLOCK
    npair = w // LANES
    gw = kv_group * MOBA_BLOCK
    tq = max(min(tq, s), gw)
    nt = s // tq
    assert nb <= LANES // 2 and s % tq == 0 and tq % gw == 0 and kbuf.shape[1:] == (batch, w, s)
    k_sel = min(MOBA_TOPK, nb)
    rows = lambda: pl.BlockSpec((tq, LANES), lambda b, p, t: (b * nt + t, p))
    feat = lambda: pl.BlockSpec((None, None, LANES, tq), lambda b, p, t: (slot, b, p, t))
    return pl.pallas_call(
        functools.partial(_moba_select_kernel, nb=nb, k_sel=k_sel),
        grid=(batch, npair, nt),
        in_specs=[rows(), rows(), rows(),
                  pl.BlockSpec((nb, LANES), lambda b, p, t: (b, p)),
                  pl.BlockSpec(memory_space=pl.ANY), pl.BlockSpec(memory_space=pl.ANY)],
        out_specs=[pl.BlockSpec((None, 2, LANES, tq), lambda b, p, t: (b, p, 0, t)),
                   pl.BlockSpec((None, 2, tq, LANES), lambda b, p, t: (b, p, t, 0)),
                   pl.BlockSpec((None, None, tq // gw, LANES + SUM_ROWS, gw), lambda b, p, t: (b, p, t, 0, 0)),
                   feat(), feat()],
        out_shape=[jax.ShapeDtypeStruct((batch, 2 * npair, LANES, s), BF16),
                   jax.ShapeDtypeStruct((batch, 2 * npair, s, LANES), BF16),
                   jax.ShapeDtypeStruct((batch, npair, s // gw, LANES + SUM_ROWS, gw), BF16),
                   jax.ShapeDtypeStruct(kbuf.shape, F32),
                   jax.ShapeDtypeStruct(vbuf.shape, F32)],
        input_output_aliases={4: 3, 5: 4},
        compiler_params=_params("parallel", "parallel", "parallel"),
        name="moba_select",
    )(q, k, v, kmeans, kbuf, vbuf)


def _moba_attn_kernel(qa_ref, ka_ref, va_ref, bias_ref, o_ref, s_a, s_b, m_a, m_b, *, kv_group, d_min, d_sat):
    s = pl.program_id(2)
    nb = pl.num_programs(2) - 1
    t = qa_ref.shape[2]
    gw = kv_group * t
    half = LANES // 2
    n_cur = jnp.where(s < nb, s // kv_group + 1, 0)
    n_prev = jnp.where(s > 0, (s + kv_group - 1) // kv_group, 0)
    n_both = jnp.minimum(n_cur, n_prev)
    qs = (qa_ref[0], qa_ref[1])

    def fold(x, op):
        return op(x.reshape(x.shape[0] // 8, 8, x.shape[1]), axis=0)

    def run(s_cur, m_cur, s_prev, m_prev):
        m_old = (m_prev[0], m_prev[1])

        def scores(g, rmax):
            start = pl.multiple_of(g * gw, gw)
            rmax = list(rmax)
            for hh in range(2):
                sc = _dot(ka_ref[hh, pl.ds(start, gw), :], qs[hh])
                for jj in range(kv_group):
                    d = jnp.minimum(s - (g * kv_group + jj), d_sat)
                    sj = sc[jj * t:(jj + 1) * t, :] + bias_ref[hh, d - d_min]
                    s_cur[hh, pl.ds(pl.multiple_of(start + jj * t, t), t), :] = sj
                    rmax[hh] = jnp.maximum(rmax[hh], fold(sj, jnp.max))
            return tuple(rmax)

        def values(g, accs):
            start = pl.multiple_of(g * gw, gw)
            vgrp = va_ref[g]
            out = []
            for hh in range(2):
                p = jnp.exp2((s_prev[hh, pl.ds(start, gw), :] - m_old[hh]).astype(BF16))
                out.append(accs[hh] + _dot(vgrp, p))
            return tuple(out)

        carry = (jnp.full((8, t), NEG_BIG, F32),) * 2 + (jnp.zeros((LANES + SUM_ROWS, t), F32),) * 2
        carry = lax.fori_loop(0, n_both, lambda g, c: scores(g, c[:2]) + values(g, c[2:]), carry)
        rmax = lax.fori_loop(n_both, n_cur, scores, carry[:2])
        acc0, acc1 = lax.fori_loop(n_both, n_prev, values, carry[2:])
        for hh in range(2):
            m_cur[hh] = jnp.max(rmax[hh], axis=0, keepdims=True)

        @pl.when(s > 0)
        def _():
            feat = lax.broadcasted_iota(jnp.int32, (LANES, t), 0)
            out_t = jnp.where(feat < half, acc0[:LANES] / acc0[LANES:LANES + 1],
                              acc1[:LANES] / acc1[LANES:LANES + 1])
            o_ref[...] = out_t.T.astype(o_ref.dtype)

    @pl.when(s % 2 == 0)
    def _():
        run(s_a, m_a, s_b, m_b)

    @pl.when(s % 2 == 1)
    def _():
        run(s_b, m_b, s_a, m_a)


def _saturation_block_distance(t, nb):
    max_exact = NUM_BUCKETS // 2
    n_sat = max_exact * (MAX_DISTANCE / max_exact) ** ((NUM_BUCKETS - 1 - max_exact) / (NUM_BUCKETS - max_exact))
    return min(math.ceil((1.02 * n_sat + t - 1) / t), nb - 1)


def moba_attn(qa, ka, va, bias, kv_group, d_min, d_sat):
    batch, nh, s, _ = ka.shape
    npair = nh // 2
    t = MOBA_BLOCK
    nb = s // t
    assert nb % kv_group == 0 and d_min <= 1 - kv_group and bias.shape[1] == d_sat - d_min + 1
    last = nb - 1
    return pl.pallas_call(
        functools.partial(_moba_attn_kernel, kv_group=kv_group, d_min=d_min, d_sat=d_sat),
        grid=(npair, batch, nb + 1),
        in_specs=[pl.BlockSpec((None, 2, LANES, t), lambda p, b, i: (b, p, 0, jnp.minimum(i, last))),
                  pl.BlockSpec((None, 2, s, LANES), lambda p, b, i: (b, p, 0, 0)),
                  pl.BlockSpec((None, None) + va.shape[2:], lambda p, b, i: (b, p, 0, 0, 0),
                               pipeline_mode=pl.Buffered(1)),
                  pl.BlockSpec((2,) + bias.shape[1:], lambda p, b, i: (p, 0, 0, 0),
                               pipeline_mode=pl.Buffered(1))],
        out_specs=pl.BlockSpec((t, LANES), lambda p, b, i: (b * nb + jnp.maximum(i - 1, 0), p)),
        out_shape=jax.ShapeDtypeStruct((batch * s, npair * LANES), BF16),
        scratch_shapes=[pltpu.VMEM((2, s, t), F32), pltpu.VMEM((2, s, t), F32),
                        pltpu.VMEM((2, 1, t), F32), pltpu.VMEM((2, 1, t), F32)],
        compiler_params=_params("parallel", "parallel", "arbitrary"),
        name="moba_attn",
    )(qa, ka, va, bias)


HEAD_GROUP = 8
SAMPLE_HEADS_PER_STEP = 4


def _sample_bias_kernel(relt_ref, o_ref, *, past):
    nh, n = o_ref.shape
    bucket = _rel_bucket(past - lax.broadcasted_iota(jnp.int32, (nh, n), 1))
    relt = relt_ref[...]
    acc = jnp.broadcast_to(relt[:, 0:1], (nh, n))
    for k in range(1, NUM_BUCKETS):
        acc = jnp.where(bucket == k, relt[:, k:k + 1], acc)
    o_ref[...] = acc


def sample_bias(rel_bias, past):
    nh = rel_bias.shape[1]
    return pl.pallas_call(
        functools.partial(_sample_bias_kernel, past=past),
        out_shape=jax.ShapeDtypeStruct((nh, past), F32),
        compiler_params=pltpu.CompilerParams(vmem_limit_bytes=VMEM_LIMIT),
        name="sample_bias",
    )(rel_bias.T)


def _sample_scores_kernel(pt_ref, qt_ref, bias_ref, *refs, scale, pages_per_block):
    page_refs, o_ref, g_ref = refs[:-2], refs[-2], refs[-1]
    nh, _, ps = page_refs[0].shape
    t = pages_per_block * ps
    for h in range(nh):
        qc = qt_ref[:, h:h + 1] * scale
        for blk in range(len(page_refs) // pages_per_block):
            tot = None
            for p in range(pages_per_block):
                sp = jnp.sum(qc * page_refs[blk * pages_per_block + p][h], axis=0, keepdims=True)
                lo = blk * t + p * ps
                o_ref[blk, h:h + 1, p * ps:(p + 1) * ps] = sp + bias_ref[h:h + 1, lo:lo + ps]
                tot = sp if tot is None else tot + sp
            g_ref[blk, h:h + 1, :] = tot


def sample_scores(qt, bias, cache_t, layer, page_table, nbc):
    bsz = page_table.shape[0]
    _, _, nh, dh, ps = cache_t.shape
    ppb = MOBA_BLOCK // ps
    assert MOBA_BLOCK == ppb * ps
    bps = math.gcd(nbc, 4)
    page = lambda i: pl.BlockSpec((None, None, nh, dh, ps),
                                  lambda b, j, pt: (layer, pt[b, j * bps * ppb + i], 0, 0, 0))
    return pl.pallas_call(
        functools.partial(_sample_scores_kernel, scale=dh ** -0.5, pages_per_block=ppb),
        grid_spec=pltpu.PrefetchScalarGridSpec(
            num_scalar_prefetch=1,
            grid=(bsz, nbc // bps),
            in_specs=[pl.BlockSpec((None, dh, nh), lambda b, j, pt: (b, 0, 0)),
                      pl.BlockSpec((nh, bps * MOBA_BLOCK), lambda b, j, pt: (0, j))]
                     + [page(i) for i in range(bps * ppb)],
            out_specs=[pl.BlockSpec((None, bps, nh, MOBA_BLOCK), lambda b, j, pt: (b, j, 0, 0)),
                       pl.BlockSpec((None, bps, nh, ps), lambda b, j, pt: (b, j, 0, 0))]),
        out_shape=[jax.ShapeDtypeStruct((bsz, nbc, nh, MOBA_BLOCK), F32),
                   jax.ShapeDtypeStruct((bsz, nbc, nh, ps), F32)],
        compiler_params=_params("parallel", "arbitrary"),
        name="sample_scores",
    )(page_table, qt, bias, *([cache_t] * (bps * ppb)))


def _sample_select_kernel(s_ref, sel_ref, *, k_sel, inv):
    nb, nh, _ = s_ref.shape
    gate = jnp.sum(s_ref[...], axis=-1) * inv
    n_avail = jnp.full((1, nh), nb, jnp.int32)
    _, picks = _top_k_rows(gate, n_avail, k_sel)
    for r, idx in enumerate(picks):
        sel_ref[r:r + 1, :] = idx.astype(jnp.int32)


def sample_select(gsum, k_sel, inv):
    bsz, nb, nh, t = gsum.shape
    scores = gsum
    return pl.pallas_call(
        functools.partial(_sample_select_kernel, k_sel=k_sel, inv=inv),
        grid=(bsz,),
        in_specs=[pl.BlockSpec((None, nb, nh, t), lambda b: (b, 0, 0, 0))],
        out_specs=pl.BlockSpec((None, k_sel, nh), lambda b: (b, 0, 0)),
        out_shape=jax.ShapeDtypeStruct((bsz, k_sel, nh), jnp.int32),
        compiler_params=_params("parallel"),
        name="sample_select",
    )(scores)


def _sample_attn_kernel(pg_ref, sel_ref, rel_ref, q_ref, kn_ref, vn_ref, *refs, nh, k_sel):
    s_ref, v_refs, o_ref = refs[0], refs[1:-1], refs[-1]
    hps = o_ref.shape[0]
    npg = len(v_refs) // hps
    ppb = npg // k_sel
    b = pl.program_id(0)
    h0 = pl.program_id(1) * hps
    dh, ps = v_refs[0].shape
    hl0 = h0 % q_ref.shape[0]
    for j in range(hps):
        h, hl = h0 + j, hl0 + j
        q = q_ref[pl.ds(hl, 1), :] * (dh ** -0.5)
        scores = [s_ref[sel_ref[(b * nh + h) * k_sel + r], pl.ds(h, 1), :] for r in range(k_sel)]
        s_self = jnp.sum(q * kn_ref[pl.ds(hl, 1), :], axis=1, keepdims=True) + rel_ref[0, h]
        mx = s_self
        for s in scores:
            mx = jnp.maximum(mx, jnp.max(s, axis=1, keepdims=True))
        p_self = jnp.exp(s_self - mx)
        den = p_self
        acc = p_self * vn_ref[pl.ds(hl, 1), :]
        for r, s in enumerate(scores):
            p = jnp.exp(s - mx)
            den = den + jnp.sum(p, axis=1, keepdims=True)
            for i in range(ppb):
                p8 = jnp.broadcast_to(p[:, i * ps:(i + 1) * ps], (8, ps))
                acc = acc + _dot_f32(p8, v_refs[j * npg + r * ppb + i][...], _dot_nt)[0:1, :]
        o_ref[j] = acc / den


def sample_attn(pages, sel_flat, rel_bias, q, k_new, v_new, scores, cache_vt, layer, k_sel):
    bsz, nh, dh = q.shape
    ps = cache_vt.shape[4]
    _, nbc, _, t = scores.shape
    ppb = t // ps
    hg = HEAD_GROUP
    hps = SAMPLE_HEADS_PER_STEP
    assert nh % hg == 0 and hg % hps == 0
    npg = ppb * k_sel

    def page(j, i):
        return pl.BlockSpec((None, None, None, dh, ps),
                            lambda b, s, pg, sel: (layer, pg[(b * nh + s * hps + j) * npg + i], s * hps + j, 0, 0))

    new = lambda: pl.BlockSpec((None, hg, dh), lambda b, s, pg, sel: (b, (s * hps) // hg, 0))
    return pl.pallas_call(
        functools.partial(_sample_attn_kernel, nh=nh, k_sel=k_sel),
        grid_spec=pltpu.PrefetchScalarGridSpec(
            num_scalar_prefetch=2,
            grid=(bsz, nh // hps),
            in_specs=[pl.BlockSpec(memory_space=pltpu.SMEM), new(), new(), new(),
                      pl.BlockSpec((None, nbc, nh, t), lambda b, s, pg, sel: (b, 0, 0, 0))]
                     + [page(j, i) for j in range(hps) for i in range(npg)],
            out_specs=pl.BlockSpec((None, hps, 1, dh), lambda b, s, pg, sel: (b, s, 0, 0))),
        out_shape=jax.ShapeDtypeStruct((bsz, nh, 1, dh), F32),
        compiler_params=_params("parallel", "arbitrary"),
        name="sample_attn",
    )(pages, sel_flat, rel_bias, q, k_new, v_new, scores, *([cache_vt] * (hps * npg)))


def _mlstm_layer(hp, hs, batch, st_c, st_n, st_m, g_mix, w_in, b_gates, g_head, w_out):
    nh = ML_HEADS
    d = hp.shape[1]
    vw = w_out.shape[0]
    dv = vw // nh
    dqk = dv // 2
    qkw = nh * dqk
    w_main = jnp.concatenate([w_in[:, :qkw], w_in[:, 2 * qkw:2 * qkw + 2 * vw]], axis=1).astype(BF16)
    wkt = w_in[:, qkw:2 * qkw].T.astype(BF16)
    wgt = w_in[:, 2 * qkw + 2 * vw:].T
    w_out_b = w_out.astype(BF16)

    q, v, o, kt, gt = ml_proj(hp, g_mix, w_main, wkt, wgt, b_gates, BF16)
    hsp, c_p, n_p, m_p = mlstm_prompt(q, v, o, kt, gt, g_head, batch)

    bs = hs.shape[0]
    q, v, o, kt, gt = ml_proj(hs, g_mix, w_main, wkt, wgt, b_gates, F32)
    hss, c_s, n_s, m_s = mlstm_sample(q.reshape(bs, nh, dqk), kt.T.reshape(bs, nh, dqk), v.reshape(bs, nh, dv),
                                      o.reshape(bs, nh, dv), gt.T.reshape(bs, 1, 2 * nh), g_head,
                                      st_c, st_n, st_m.reshape(bs, 1, nh))
    return hsp, hss.reshape(bs, vw), w_out_b, (c_p, n_p, m_p), (c_s, n_s, m_s.reshape(bs, nh))


def _moba_layer(hp, hs, batch, cache_kt, cache_vt, layer, page_table, bias, kv_group, rel_bias, g_mix, w_qkv,
                w_out, kbuf, vbuf):
    _, _, nh, dh, ps = cache_kt.shape
    w = w_qkv.shape[1] // 3
    assert dh == LANES // 2 and w == nh * dh
    w_qkv_b = w_qkv.astype(BF16)
    w_out_b = w_out.astype(BF16)

    q, k, v = norm_proj(hp, g_mix, w_qkv_b, (w, w, w))
    kmeans = block_means(k)
    qa, ka, va, kbuf, vbuf = moba_select(q, k, v, kmeans, batch, layer, kbuf, vbuf, kv_group)
    o = moba_attn(qa, ka, va, bias[0], kv_group, bias[1], bias[2])

    bs, n_pages = page_table.shape
    past = n_pages * ps
    sbias = bias[3]
    assert sbias.shape == (nh, past)
    nbc = past // MOBA_BLOCK
    assert past == nbc * MOBA_BLOCK and nbc >= 1, "a partially filled tail block is not supported"
    k_sel = min(MOBA_TOPK, nbc)
    (qkv_s,) = norm_proj(hs, g_mix, w_qkv_b, (3 * w,))
    q_s, k_s, v_s = (qkv_s[:, i * w:(i + 1) * w].reshape(bs, nh, dh) for i in range(3))
    ppb = MOBA_BLOCK // ps
    scores, gsum = sample_scores(jnp.transpose(q_s, (0, 2, 1)), sbias, cache_kt, layer, page_table, nbc)
    sel = sample_select(gsum, k_sel, 1.0 / (MOBA_BLOCK * dh ** -0.5))
    sel_bhk = jnp.transpose(sel, (0, 2, 1))
    logical = sel_bhk[..., None] * ppb + jnp.arange(ppb, dtype=jnp.int32)
    pages = jnp.take_along_axis(page_table, logical.reshape(bs, -1), axis=1)
    o_s = sample_attn(pages.reshape(-1), sel_bhk.reshape(-1), rel_bias, q_s, k_s, v_s,
                      scores, cache_vt, layer, k_sel)
    return o, o_s.reshape(bs, w), w_out_b, kbuf, vbuf, (k_s, v_s)


def kernel(x_prompt, x_sample, cache_k, cache_v, state_C, state_n, state_m, page_table, norm_mix, norm_ffn,
           norm_final, rel_bias, w_ml_in, b_ml_gates, g_ml_head, w_ml_out, w_attn_qkv, w_attn_out, w_ffn_in,
           w_ffn_out):
    batch, seq, d = x_prompt.shape
    bs, dec_seq, _ = x_sample.shape
    assert dec_seq == 1 and seq % MOBA_BLOCK == 0
    depth = norm_mix.shape[0]
    nh_at = cache_k.shape[3]
    dh = cache_k.shape[4]
    hp = x_prompt.reshape(batch * seq, d)
    hs = x_sample.reshape(bs, d)
    w_ffn_in_b = w_ffn_in.astype(BF16)
    w_ffn_out_b = w_ffn_out.astype(BF16)
    nb = seq // MOBA_BLOCK
    kv_group = math.gcd(nb, KV_GROUP)
    d_min, d_sat = 1 - kv_group, _saturation_block_distance(MOBA_BLOCK, nb)
    bias = (bias_tiles(rel_bias, d_min, d_sat, MOBA_BLOCK), d_min, d_sat,
            sample_bias(rel_bias, page_table.shape[1] * cache_k.shape[2]))
    cache_kt = jnp.transpose(cache_k, (0, 1, 3, 4, 2))
    cache_vt = jnp.transpose(cache_v, (0, 1, 3, 4, 2))

    kbuf = jnp.zeros((depth // 2, batch, nh_at * dh, seq), F32)
    vbuf = jnp.zeros((depth // 2, batch, nh_at * dh, seq), F32)
    kv_s, st_p, st_s = [], [], []
    for layer in range(depth):
        j = layer // 2
        if layer % 2 == 0:
            ap, asm, w_o, sp, ss = _mlstm_layer(hp, hs, batch, state_C[j], state_n[j], state_m[j], norm_mix[layer],
                                                w_ml_in[j], b_ml_gates[j], g_ml_head[j], w_ml_out[j])
            st_p.append(sp)
            st_s.append(ss)
        else:
            ap, asm, w_o, kbuf, vbuf, ks = _moba_layer(hp, hs, batch, cache_kt, cache_vt, j, page_table, bias,
                                                       kv_group, rel_bias, norm_mix[layer], w_attn_qkv[j],
                                                       w_attn_out[j], kbuf, vbuf)
            kv_s.append(ks)
        g_last = norm_final if layer == depth - 1 else None
        hp = proj_ffn(ap, w_o, hp, norm_ffn[layer], w_ffn_in_b[layer], w_ffn_out_b[layer], g_last)
        hs = proj_ffn(asm, w_o, hs, norm_ffn[layer], w_ffn_in_b[layer], w_ffn_out_b[layer], g_last)

    y_prompt = hp.reshape(batch, seq, d)
    y_sample = hs.reshape(bs, dec_seq, d)
    stack = lambda items, i, shape: jnp.stack([it[i] for it in items]).reshape((len(items),) + shape)
    to_rows = lambda buf: jnp.transpose(buf.reshape(buf.shape[0], batch, nh_at, dh, seq), (0, 1, 4, 2, 3))
    return (y_prompt, y_sample, to_rows(kbuf), to_rows(vbuf),
            stack(kv_s, 0, (bs, dec_seq, nh_at, dh)), stack(kv_s, 1, (bs, dec_seq, nh_at, dh)),
            jnp.stack([s[0] for s in st_p]), jnp.stack([s[1] for s in st_p]), jnp.stack([s[2] for s in st_p]),
            jnp.stack([s[0] for s in st_s]), jnp.stack([s[1] for s in st_s]), jnp.stack([s[2] for s in st_s]))
```

```python
import functools
import math

import jax
import jax.numpy as jnp
from jax import lax
from jax.experimental import pallas as pl
from jax.experimental.pallas import tpu as pltpu

F32 = jnp.float32
BF16 = jnp.bfloat16

EPS = 1e-6
ML_HEADS = 8
ML_CHUNK = 256
MOBA_BLOCK = 256
MOBA_TOPK = 3
KV_GROUP = 4
FFN_CHUNK = 256
SUM_ROWS = 16
LOG2E = 1.4426950408889634
NUM_BUCKETS = 32
MAX_DISTANCE = 4096
NEG_BIG = -1e30
LANES = 128
VMEM_LIMIT = 56 * 1024 * 1024


def _dot(a, b):
    return jnp.dot(a, b, preferred_element_type=F32)


def _dot_nt(a, b):
    return lax.dot_general(a, b, (((1,), (1,)), ((), ())), preferred_element_type=F32)


def _split2(x):
    hi = x.astype(BF16)
    lo = (x - hi.astype(F32)).astype(BF16)
    return hi, lo


def _split3(x):
    hi = x.astype(BF16)
    r = x - hi.astype(F32)
    mid = r.astype(BF16)
    lo = (r - mid.astype(F32)).astype(BF16)
    return hi, mid, lo


def _dot_f32(a, b, dot=_dot):
    ah, al = _split2(a)
    bh, bl = _split2(b)
    return dot(ah, bh) + dot(ah, bl) + dot(al, bh)


def _norm_rows(x, g):
    return x * lax.rsqrt(jnp.mean(x * x, axis=-1, keepdims=True) + EPS) * g


def _log_sigmoid(x):
    return jnp.minimum(x, 0.0) - jnp.log1p(jnp.exp(-jnp.abs(x)))


def _rel_bucket(dist):
    n = jnp.maximum(dist, 0)
    max_exact = NUM_BUCKETS // 2
    nf = jnp.maximum(n, 1).astype(F32)
    large = max_exact + (jnp.log(nf / max_exact) / math.log(MAX_DISTANCE / max_exact)
                         * (NUM_BUCKETS - max_exact)).astype(jnp.int32)
    return jnp.where(n < max_exact, n, jnp.minimum(large, NUM_BUCKETS - 1))


def _bias_lookup(bucket, rel_ref, h):
    acc = jnp.full(bucket.shape, rel_ref[0, h], F32)
    for k in range(1, NUM_BUCKETS):
        acc = jnp.where(bucket == k, rel_ref[k, h], acc)
    return acc


def _params(*sem):
    return pltpu.CompilerParams(dimension_semantics=sem, vmem_limit_bytes=VMEM_LIMIT)


def _row_tile(m, target):
    t = min(m, target)
    assert m % t == 0, (m, t)
    return t


def _proj_kernel(x_ref, g_ref, w_ref, *o_refs):
    xb = _norm_rows(x_ref[...], g_ref[...]).astype(BF16)
    c0 = 0
    for o_ref in o_refs:
        n = o_ref.shape[1]
        for c in range(0, n, 512):
            w = min(512, n - c)
            o_ref[:, c:c + w] = _dot(xb, w_ref[:, c0 + c:c0 + c + w])
        c0 += n


def norm_proj(x, g, w, widths, tm=512):
    m, d = x.shape
    tm = _row_tile(m, tm)
    n = w.shape[1]
    assert sum(widths) == n
    return pl.pallas_call(
        _proj_kernel,
        grid=(m // tm,),
        in_specs=[pl.BlockSpec((tm, d), lambda i: (i, 0)),
                  pl.BlockSpec((1, d), lambda i: (0, 0)),
                  pl.BlockSpec((d, n), lambda i: (0, 0))],
        out_specs=[pl.BlockSpec((tm, wd), lambda i: (i, 0)) for wd in widths],
        out_shape=[jax.ShapeDtypeStruct((m, wd), F32) for wd in widths],
        compiler_params=_params("parallel"),
        name="norm_proj",
    )(x, g.reshape(1, d), w)


def _ml_proj_kernel(x_ref, g_ref, w_ref, wkt_ref, wgt_ref, bg_ref, q_ref, v_ref, o_ref, kt_ref, gt_ref):
    xn = _norm_rows(x_ref[...], g_ref[...])
    xb = xn.astype(BF16)
    xlo = (xn - xb.astype(F32)).astype(BF16)
    c0 = 0
    for out in (q_ref, v_ref, o_ref):
        n = out.shape[1]
        for c in range(0, n, 512):
            out[:, c:c + 512] = _dot(xb, w_ref[:, c0 + c:c0 + c + 512]).astype(out.dtype)
        c0 += n
    kt_ref[...] = _dot_nt(wkt_ref[...], xb)
    gh, gl = _split2(wgt_ref[...])
    gt_ref[...] = _dot_nt(gh, xb) + _dot_nt(gh, xlo) + _dot_nt(gl, xb) + bg_ref[...]


def ml_proj(x, g, w_main, wkt, wgt, bg, row_dtype, tm=512):
    m, d = x.shape
    tm = _row_tile(m, tm)
    qk = wkt.shape[0]
    vw = (w_main.shape[1] - qk) // 2
    ng = wgt.shape[0]
    return pl.pallas_call(
        _ml_proj_kernel,
        grid=(m // tm,),
        in_specs=[pl.BlockSpec((tm, d), lambda i: (i, 0)),
                  pl.BlockSpec((1, d), lambda i: (0, 0)),
                  pl.BlockSpec(w_main.shape, lambda i: (0, 0)),
                  pl.BlockSpec(wkt.shape, lambda i: (0, 0)),
                  pl.BlockSpec(wgt.shape, lambda i: (0, 0)),
                  pl.BlockSpec((ng, 1), lambda i: (0, 0))],
        out_specs=[pl.BlockSpec((tm, qk), lambda i: (i, 0)),
                   pl.BlockSpec((tm, vw), lambda i: (i, 0)),
                   pl.BlockSpec((tm, vw), lambda i: (i, 0)),
                   pl.BlockSpec((qk, tm), lambda i: (0, i)),
                   pl.BlockSpec((ng, tm), lambda i: (0, i))],
        out_shape=[jax.ShapeDtypeStruct((m, qk), row_dtype),
                   jax.ShapeDtypeStruct((m, vw), row_dtype),
                   jax.ShapeDtypeStruct((m, vw), row_dtype),
                   jax.ShapeDtypeStruct((qk, m), F32),
                   jax.ShapeDtypeStruct((ng, m), F32)],
        compiler_params=_params("parallel"),
        name="ml_proj",
    )(x, g.reshape(1, d), w_main, wkt, wgt, bg.reshape(ng, 1))


def _proj_ffn_kernel(a_ref, wm_ref, r_ref, g_ref, win_ref, wo_ref, *rest):
    gf_ref = rest[0] if len(rest) == 4 else None
    o_ref, xn_s, acc_s = rest[-3:]
    h = r_ref[...] + _dot(a_ref[...].astype(BF16), wm_ref[...])
    xn_s[...] = _norm_rows(h, g_ref[...]).astype(BF16)
    acc_s[...] = h

    def chunk(f, carry):
        xb = xn_s[...]
        gp = _dot(xb, win_ref[0, f])
        up = _dot(xb, win_ref[1, f])
        act = (gp * jax.nn.sigmoid(gp) * up).astype(BF16)
        acc_s[...] += _dot(act, wo_ref[f])
        return carry

    lax.fori_loop(0, wo_ref.shape[0], chunk, 0)
    out = acc_s[...]
    o_ref[...] = out if gf_ref is None else _norm_rows(out, gf_ref[...])


def proj_ffn(a, w_mix, res, g, w_in, w_out, layer, g_final=None, tm=1024):
    m, k = a.shape
    d = res.shape[1]
    tm = _row_tile(m, tm)
    once = pl.Buffered(1)
    row = lambda: pl.BlockSpec((1, d), lambda i: (0, 0))
    in_specs = [pl.BlockSpec((tm, k), lambda i: (i, 0)),
                pl.BlockSpec((k, d), lambda i: (0, 0), pipeline_mode=once),
                pl.BlockSpec((tm, d), lambda i: (i, 0)),
                row(),
                pl.BlockSpec((None,) + w_in.shape[1:], lambda i: (layer, 0, 0, 0, 0), pipeline_mode=once),
                pl.BlockSpec((None,) + w_out.shape[1:], lambda i: (layer, 0, 0, 0), pipeline_mode=once)]
    args = [a, w_mix, res, g.reshape(1, d), w_in, w_out]
    if g_final is not None:
        in_specs.append(row())
        args.append(g_final.reshape(1, d))
    return pl.pallas_call(
        _proj_ffn_kernel,
        grid=(m // tm,),
        in_specs=in_specs,
        out_specs=pl.BlockSpec((tm, d), lambda i: (i, 0)),
        out_shape=jax.ShapeDtypeStruct((m, d), F32),
        scratch_shapes=[pltpu.VMEM((tm, d), BF16), pltpu.VMEM((tm, d), F32)],
        compiler_params=_params("parallel"),
        name="proj_ffn",
    )(*args)


def _mlstm_chunk_kernel(q_ref, v_ref, o_ref, kt_ref, gt_ref, gh_ref, hs_ref, caug_ref, m_ref, caug_s, m_s):
    c = pl.program_id(1)
    L = q_ref.shape[0]
    nh = caug_s.shape[0]
    dqk = caug_s.shape[1]
    dv = caug_s.shape[2] // 2

    @pl.when(c == 0)
    def _():
        caug_s[...] = jnp.zeros_like(caug_s)
        m_s[...] = jnp.zeros_like(m_s)

    row = lax.broadcasted_iota(jnp.int32, (L, L), 0)
    col = lax.broadcasted_iota(jnp.int32, (L, L), 1)
    causal = col <= row
    lower = jnp.where(causal, 1.0, 0.0).astype(BF16)
    upper = jnp.where(row <= col, 1.0, 0.0).astype(BF16)
    gt = gt_ref[...]
    ig = gt[0:nh, :]
    logf = _log_sigmoid(gt[nh:2 * nh, :])
    parts = _split3(logf)
    f_row = _dot(parts[0], upper) + _dot(parts[1], upper) + _dot(parts[2], upper)
    f_col = _dot_nt(lower, parts[0]) + _dot_nt(lower, parts[1]) + _dot_nt(lower, parts[2])
    a_rows = ig - f_row
    ones_col = jnp.where(lax.broadcasted_iota(jnp.int32, (L, dv), 1) == 0, 1.0, 0.0).astype(BF16)

    for h in range(nh):
        m_prev = m_s[h:h + 1, 0:1]
        a_row = a_rows[h:h + 1, :]
        amat = jnp.where(causal, a_row, -jnp.inf)
        g = jnp.maximum(jnp.max(amat, axis=1, keepdims=True), m_prev)
        w = jnp.exp(amat - g)
        qh = q_ref[:, h * dqk:(h + 1) * dqk].astype(BF16)
        kth = kt_ref[h * dqk:(h + 1) * dqk, :] * (dqk ** -0.5)
        p = (_dot(qh, kth.astype(BF16)) * w).astype(BF16)
        vaug = jnp.concatenate([v_ref[:, h * dv:(h + 1) * dv].astype(BF16), ones_col], axis=1)
        caug = caug_s[h]
        a_int = jnp.exp(m_prev - g)
        tot = a_int * _dot(qh, caug.astype(BF16)) + _dot(p, vaug)
        m_t = f_col[:, h:h + 1] + g
        hh = tot[:, :dv] / jnp.maximum(jnp.abs(tot[:, dv:dv + 1]), jnp.exp(-m_t))
        hn = hh * lax.rsqrt(jnp.mean(hh * hh, axis=-1, keepdims=True) + EPS) * gh_ref[:, h * dv:(h + 1) * dv]
        hs_ref[:, h * dv:(h + 1) * dv] = (hn * jax.nn.sigmoid(o_ref[:, h * dv:(h + 1) * dv].astype(F32))).astype(hs_ref.dtype)
        g_last = g[L - 1:L, :]
        w_last = jnp.exp(a_row - g_last)
        caug_s[h] = jnp.exp(m_prev - g_last) * caug + _dot((kth * w_last).astype(BF16), vaug)
        m_s[h:h + 1, :] = jnp.broadcast_to(f_row[h:h + 1, L - 1:L] + g_last, (1, m_s.shape[1]))

    @pl.when(c == pl.num_programs(1) - 1)
    def _():
        caug_ref[...] = caug_s[...]
        m_ref[...] = m_s[...]


def mlstm_prompt(q, v, o, kt, gt, g_head, batch):
    m, qkw = q.shape
    vw = v.shape[1]
    nh = gt.shape[0] // 2
    dqk, dv = qkw // nh, vw // nh
    s = m // batch
    L = math.gcd(s, ML_CHUNK)
    nc = s // L
    hs, caug, mm = pl.pallas_call(
        _mlstm_chunk_kernel,
        grid=(batch, nc),
        in_specs=[pl.BlockSpec((L, qkw), lambda b, c: (b * nc + c, 0)),
                  pl.BlockSpec((L, vw), lambda b, c: (b * nc + c, 0)),
                  pl.BlockSpec((L, vw), lambda b, c: (b * nc + c, 0)),
                  pl.BlockSpec((qkw, L), lambda b, c: (0, b * nc + c)),
                  pl.BlockSpec((2 * nh, L), lambda b, c: (0, b * nc + c)),
                  pl.BlockSpec((1, vw), lambda b, c: (0, 0))],
        out_specs=[pl.BlockSpec((L, vw), lambda b, c: (b * nc + c, 0)),
                   pl.BlockSpec((None, nh, dqk, 2 * dv), lambda b, c: (b, 0, 0, 0)),
                   pl.BlockSpec((None, nh, LANES), lambda b, c: (b, 0, 0))],
        out_shape=[jax.ShapeDtypeStruct((m, vw), BF16),
                   jax.ShapeDtypeStruct((batch, nh, dqk, 2 * dv), F32),
                   jax.ShapeDtypeStruct((batch, nh, LANES), F32)],
        scratch_shapes=[pltpu.VMEM((nh, dqk, 2 * dv), F32), pltpu.VMEM((nh, LANES), F32)],
        compiler_params=_params("parallel", "arbitrary"),
        name="mlstm_prompt",
    )(q, v, o, kt, gt, g_head.reshape(1, vw))
    return hs, caug[..., :dv], caug[..., dv], mm[..., 0]


def _mlstm_step_kernel(q_ref, k_ref, v_ref, o_ref, gt_ref, m_ref, gh_ref, c_ref, n_ref,
                       hs_ref, cn_ref, nn_ref, mn_ref):
    nh, dqk, dv = c_ref.shape
    scale = dqk ** -0.5
    q = q_ref[...]
    k = k_ref[...] * scale
    eye = jnp.where(lax.broadcasted_iota(jnp.int32, (dqk, dqk), 0)
                    == lax.broadcasted_iota(jnp.int32, (dqk, dqk), 1), 1.0, 0.0).astype(BF16)
    q3, k3 = _split3(q), _split3(k)
    q_cols = _dot_nt(eye, q3[0]) + _dot_nt(eye, q3[1]) + _dot_nt(eye, q3[2])
    k_cols = _dot_nt(eye, k3[0]) + _dot_nt(eye, k3[1]) + _dot_nt(eye, k3[2])
    gt = gt_ref[...]
    for h in range(nh):
        ig = gt[:, h:h + 1]
        logf = _log_sigmoid(gt[:, nh + h:nh + h + 1])
        m_inter = m_ref[:, h:h + 1] + logf
        m_t = jnp.maximum(m_inter, ig)
        w = jnp.exp(ig - m_t)
        a = jnp.exp(m_inter - m_t)
        q_row, k_row = q[h:h + 1, :], k[h:h + 1, :]
        q_col, k_col = q_cols[:, h:h + 1], k_cols[:, h:h + 1]
        v_row = v_ref[h:h + 1, :]
        c_h = c_ref[h]
        n_row = n_ref[h:h + 1, :]
        sqk = jnp.sum(q_row * k_row, axis=1, keepdims=True) * w
        num = a * jnp.sum(q_col * c_h, axis=0, keepdims=True) + sqk * v_row
        den = a * jnp.sum(q_row * n_row, axis=1, keepdims=True) + sqk
        hh = num / jnp.maximum(jnp.abs(den), jnp.exp(-m_t))
        hn = hh * lax.rsqrt(jnp.mean(hh * hh, axis=-1, keepdims=True) + EPS) * gh_ref[h:h + 1, :]
        hs_ref[h:h + 1, :] = (hn * jax.nn.sigmoid(o_ref[h:h + 1, :])).astype(hs_ref.dtype)
        cn_ref[h] = a * c_h + (w * k_col) * v_row
        nn_ref[h:h + 1, :] = a * n_row + w * k_row
        mn_ref[:, h:h + 1] = m_t


def mlstm_sample(q, k, v, o, gates, g_head, c0, n0, m0):
    bsz, nh, dqk, dv = c0.shape
    row3 = lambda w: pl.BlockSpec((None, nh, w), lambda b: (b, 0, 0))
    return pl.pallas_call(
        _mlstm_step_kernel,
        grid=(bsz,),
        in_specs=[row3(dqk), row3(dqk), row3(dv), row3(dv),
                  pl.BlockSpec((None, 1, 2 * nh), lambda b: (b, 0, 0)),
                  pl.BlockSpec((None, 1, nh), lambda b: (b, 0, 0)),
                  pl.BlockSpec((nh, dv), lambda b: (0, 0)),
                  pl.BlockSpec((None, nh, dqk, dv), lambda b: (b, 0, 0, 0)),
                  row3(dqk)],
        out_specs=[row3(dv),
                   pl.BlockSpec((None, nh, dqk, dv), lambda b: (b, 0, 0, 0)),
                   row3(dqk),
                   pl.BlockSpec((None, 1, nh), lambda b: (b, 0, 0))],
        out_shape=[jax.ShapeDtypeStruct((bsz, nh, dv), F32),
                   jax.ShapeDtypeStruct((bsz, nh, dqk, dv), F32),
                   jax.ShapeDtypeStruct((bsz, nh, dqk), F32),
                   jax.ShapeDtypeStruct((bsz, 1, nh), F32)],
        compiler_params=_params("parallel"),
        name="mlstm_sample",
    )(q, k, v, o, gates, m0, g_head.reshape(nh, dv), c0, n0)


def _bucket_bounds(lo, hi):
    def bucket(n):
        n = max(n, 0)
        max_exact = NUM_BUCKETS // 2
        if n < max_exact:
            return n
        return min(max_exact + int(math.log(n / max_exact) / math.log(MAX_DISTANCE / max_exact)
                                   * (NUM_BUCKETS - max_exact)), NUM_BUCKETS - 1)
    return max(bucket(lo) - 1, 0), min(bucket(hi) + 1, NUM_BUCKETS - 1)


def _bias_tiles_kernel(kr_ref, rel_ref, o_ref, *, d_min):
    step = pl.program_id(0)
    d = step + d_min
    nh, t, _ = o_ref.shape
    dist = d * t + lax.broadcasted_iota(jnp.int32, (t, t), 1) - lax.broadcasted_iota(jnp.int32, (t, t), 0)
    bucket = _rel_bucket(dist)
    k_lo, k_hi = kr_ref[2 * step], kr_ref[2 * step + 1]
    for h in range(nh):
        acc = lax.fori_loop(k_lo + 1, k_hi + 1, lambda k, a: jnp.where(bucket == k, rel_ref[k, h], a),
                            jnp.full((t, t), rel_ref[k_lo, h], F32))
        o_ref[h] = jnp.where(dist >= 0, acc * LOG2E, NEG_BIG)


def bias_tiles(rel_bias, d_min, d_max, t):
    nh = rel_bias.shape[1]
    nd = d_max - d_min + 1
    bounds = [b for d in range(d_min, d_max + 1) for b in _bucket_bounds(d * t - (t - 1), d * t + (t - 1))]
    return pl.pallas_call(
        functools.partial(_bias_tiles_kernel, d_min=d_min),
        grid_spec=pltpu.PrefetchScalarGridSpec(
            num_scalar_prefetch=1,
            grid=(nd,),
            in_specs=[pl.BlockSpec(memory_space=pltpu.SMEM)],
            out_specs=pl.BlockSpec((nh, None, t, t), lambda d, kr: (0, d, 0, 0))),
        out_shape=jax.ShapeDtypeStruct((nh, nd, t, t), F32),
        compiler_params=_params("parallel"),
        name="bias_tiles",
    )(jnp.asarray(bounds, jnp.int32), rel_bias)


def _kmeans_kernel(k_ref, o_ref):
    nblk = o_ref.shape[0]
    t = k_ref.shape[0] // nblk
    for j in range(nblk):
        o_ref[j:j + 1, :] = jnp.sum(k_ref[j * t:(j + 1) * t, :], axis=0, keepdims=True) * (1.0 / t)


def block_means(k, blocks_per_step=8):
    m, w = k.shape
    rows = MOBA_BLOCK * blocks_per_step
    assert m % rows == 0
    return pl.pallas_call(
        _kmeans_kernel,
        grid=(m // rows,),
        in_specs=[pl.BlockSpec((rows, w), lambda i: (i, 0))],
        out_specs=pl.BlockSpec((blocks_per_step, w), lambda i: (i, 0)),
        out_shape=jax.ShapeDtypeStruct((m // MOBA_BLOCK, w), F32),
        compiler_params=_params("parallel"),
        name="block_means",
    )(k)


def _top_k_rows(gate, n_avail, k_sel):
    nb = gate.shape[0]
    jidx = lax.broadcasted_iota(jnp.int32, gate.shape, 0).astype(F32)
    selected = jnp.zeros(gate.shape, F32)
    picks = []
    for r in range(k_sel):
        mx = jnp.max(gate, axis=0, keepdims=True)
        idx = jnp.min(jnp.where(gate == mx, jidx, float(nb)), axis=0, keepdims=True)
        hit = jidx == idx
        selected = jnp.where(jnp.logical_and(hit, r < n_avail), 1.0, selected)
        gate = jnp.where(hit, -jnp.inf, gate)
        picks.append(idx)
    return selected > 0.5, picks


def _moba_select_kernel(q_ref, k_ref, v_ref, km_ref, kbuf_ref, vbuf_ref, qa_ref, ka_ref, va_ref, kt_ref, vt_ref,
                        *, nb, k_sel):
    del kbuf_ref, vbuf_ref
    t = pl.program_id(2)
    tq = q_ref.shape[0]
    half = LANES // 2
    shift = MOBA_BLOCK.bit_length() - 1
    q2, k2 = q_ref[...], k_ref[...]
    qt = q2.T * (half ** -0.5 * LOG2E)
    vt = v_ref[...].T
    kt_ref[...] = k2.T
    vt_ref[...] = vt
    ngrp, _, gw = va_ref.shape
    va = jnp.concatenate([vt.astype(BF16), jnp.ones((SUM_ROWS, tq), BF16)], axis=0)
    for gi in range(ngrp):
        va_ref[gi] = va[:, gi * gw:(gi + 1) * gw]
    km = km_ref[...]
    lane = lax.broadcasted_iota(jnp.int32, (tq, LANES), 1)
    rowblk = (t * tq + lax.broadcasted_iota(jnp.int32, (tq, LANES), 0)) >> shift
    feat = lax.broadcasted_iota(jnp.int32, (LANES, tq), 0)
    km_lane = lax.broadcasted_iota(jnp.int32, km.shape, 1)
    pos = t * tq + lax.broadcasted_iota(jnp.int32, (1, tq), 1)
    n_avail = jnp.minimum(pos >> shift, nb)
    jidx = lax.broadcasted_iota(jnp.int32, (nb, tq), 0)
    for hh in range(2):
        off = 0 if hh else half
        kmh = jnp.where((km_lane >= half) if hh else (km_lane < half), km, 0.0)
        gate = _dot_f32(kmh, q2, _dot_nt)
        gate = jnp.where(jidx < n_avail, gate, -jnp.inf)
        selected, _ = _top_k_rows(gate, n_avail, k_sel)
        code = jnp.where(jnp.logical_or(selected, jidx == (pos >> shift)), 0.0, NEG_BIG)
        pieces = [code, jnp.zeros((LANES - nb - off, tq), F32)]
        if off:
            pieces = [jnp.zeros((off, tq), F32)] + pieces
        code_rows = jnp.concatenate(pieces, axis=0)
        qa_ref[hh] = jnp.where((feat >= half) if hh else (feat < half), qt, code_rows).astype(BF16)
        onehot = jnp.where((lane - off) == rowblk, 1.0, 0.0)
        ka_ref[hh] = jnp.where((lane >= half) if hh else (lane < half), k2, onehot).astype(BF16)


def moba_select(q, k, v, kmeans, batch, slot, kbuf, vbuf, kv_group, tq=2048):
    m, w = q.shape
    s = m // batch
    nb = s // MOBA_BLOCK
    npair = w // LANES
    gw = kv_group * MOBA_BLOCK
    tq = max(min(tq, s), gw)
    nt = s // tq
    assert nb <= LANES // 2 and s % tq == 0 and tq % gw == 0 and kbuf.shape[1:] == (batch, w, s)
    k_sel = min(MOBA_TOPK, nb)
    rows = lambda: pl.BlockSpec((tq, LANES), lambda b, p, t: (b * nt + t, p))
    feat = lambda: pl.BlockSpec((None, None, LANES, tq), lambda b, p, t: (slot, b, p, t))
    return pl.pallas_call(
        functools.partial(_moba_select_kernel, nb=nb, k_sel=k_sel),
        grid=(batch, npair, nt),
        in_specs=[rows(), rows(), rows(),
                  pl.BlockSpec((nb, LANES), lambda b, p, t: (b, p)),
                  pl.BlockSpec(memory_space=pl.ANY), pl.BlockSpec(memory_space=pl.ANY)],
        out_specs=[pl.BlockSpec((None, 2, LANES, tq), lambda b, p, t: (b, p, 0, t)),
                   pl.BlockSpec((None, 2, tq, LANES), lambda b, p, t: (b, p, t, 0)),
                   pl.BlockSpec((None, None, tq // gw, LANES + SUM_ROWS, gw), lambda b, p, t: (b, p, t, 0, 0)),
                   feat(), feat()],
        out_shape=[jax.ShapeDtypeStruct((batch, 2 * npair, LANES, s), BF16),
                   jax.ShapeDtypeStruct((batch, 2 * npair, s, LANES), BF16),
                   jax.ShapeDtypeStruct((batch, npair, s // gw, LANES + SUM_ROWS, gw), BF16),
                   jax.ShapeDtypeStruct(kbuf.shape, F32),
                   jax.ShapeDtypeStruct(vbuf.shape, F32)],
        input_output_aliases={4: 3, 5: 4},
        compiler_params=_params("parallel", "parallel", "parallel"),
        name="moba_select",
    )(q, k, v, kmeans, kbuf, vbuf)


def _moba_attn_kernel(qa_ref, ka_ref, va_ref, bias_ref, o_ref, s_a, s_b, m_a, m_b, *, kv_group, d_min, d_sat):
    s = pl.program_id(2)
    nb = pl.num_programs(2) - 1
    t = qa_ref.shape[2]
    gw = kv_group * t
    half = LANES // 2
    n_cur = jnp.where(s < nb, s // kv_group + 1, 0)
    n_prev = jnp.where(s > 0, (s + kv_group - 1) // kv_group, 0)
    n_both = jnp.minimum(n_cur, n_prev)
    qs = (qa_ref[0], qa_ref[1])

    def fold(x, op):
        return op(x.reshape(x.shape[0] // 8, 8, x.shape[1]), axis=0)

    def run(s_cur, m_cur, s_prev, m_prev):
        m_old = (m_prev[0], m_prev[1])

        def scores(g, rmax):
            start = pl.multiple_of(g * gw, gw)
            rmax = list(rmax)
            for hh in range(2):
                sc = _dot(ka_ref[hh, pl.ds(start, gw), :], qs[hh])
                for jj in range(kv_group):
                    d = jnp.minimum(s - (g * kv_group + jj), d_sat)
                    sj = sc[jj * t:(jj + 1) * t, :] + bias_ref[hh, d - d_min]
                    s_cur[hh, pl.ds(pl.multiple_of(start + jj * t, t), t), :] = sj
                    rmax[hh] = jnp.maximum(rmax[hh], fold(sj, jnp.max))
            return tuple(rmax)

        def values(g, accs):
            start = pl.multiple_of(g * gw, gw)
            vgrp = va_ref[g]
            out = []
            for hh in range(2):
                p = jnp.exp2((s_prev[hh, pl.ds(start, gw), :] - m_old[hh]).astype(BF16))
                out.append(accs[hh] + _dot(vgrp, p))
            return tuple(out)

        carry = (jnp.full((8, t), NEG_BIG, F32),) * 2 + (jnp.zeros((LANES + SUM_ROWS, t), F32),) * 2
        carry = lax.fori_loop(0, n_both, lambda g, c: scores(g, c[:2]) + values(g, c[2:]), carry)
        rmax = lax.fori_loop(n_both, n_cur, scores, carry[:2])
        acc0, acc1 = lax.fori_loop(n_both, n_prev, values, carry[2:])
        for hh in range(2):
            m_cur[hh] = jnp.max(rmax[hh], axis=0, keepdims=True)

        @pl.when(s > 0)
        def _():
            feat = lax.broadcasted_iota(jnp.int32, (LANES, t), 0)
            out_t = jnp.where(feat < half, acc0[:LANES] / acc0[LANES:LANES + 1],
                              acc1[:LANES] / acc1[LANES:LANES + 1])
            o_ref[...] = out_t.T.astype(o_ref.dtype)

    @pl.when(s % 2 == 0)
    def _():
        run(s_a, m_a, s_b, m_b)

    @pl.when(s % 2 == 1)
    def _():
        run(s_b, m_b, s_a, m_a)


def _saturation_block_distance(t, nb):
    max_exact = NUM_BUCKETS // 2
    n_sat = max_exact * (MAX_DISTANCE / max_exact) ** ((NUM_BUCKETS - 1 - max_exact) / (NUM_BUCKETS - max_exact))
    return min(math.ceil((1.02 * n_sat + t - 1) / t), nb - 1)


def moba_attn(qa, ka, va, bias, kv_group, d_min, d_sat):
    batch, nh, s, _ = ka.shape
    npair = nh // 2
    t = MOBA_BLOCK
    nb = s // t
    assert nb % kv_group == 0 and d_min <= 1 - kv_group and bias.shape[1] == d_sat - d_min + 1
    last = nb - 1
    return pl.pallas_call(
        functools.partial(_moba_attn_kernel, kv_group=kv_group, d_min=d_min, d_sat=d_sat),
        grid=(npair, batch, nb + 1),
        in_specs=[pl.BlockSpec((None, 2, LANES, t), lambda p, b, i: (b, p, 0, jnp.minimum(i, last))),
                  pl.BlockSpec((None, 2, s, LANES), lambda p, b, i: (b, p, 0, 0)),
                  pl.BlockSpec((None, None) + va.shape[2:], lambda p, b, i: (b, p, 0, 0, 0),
                               pipeline_mode=pl.Buffered(1)),
                  pl.BlockSpec((2,) + bias.shape[1:], lambda p, b, i: (p, 0, 0, 0),
                               pipeline_mode=pl.Buffered(1))],
        out_specs=pl.BlockSpec((t, LANES), lambda p, b, i: (b * nb + jnp.maximum(i - 1, 0), p)),
        out_shape=jax.ShapeDtypeStruct((batch * s, npair * LANES), BF16),
        scratch_shapes=[pltpu.VMEM((2, s, t), F32), pltpu.VMEM((2, s, t), F32),
                        pltpu.VMEM((2, 1, t), F32), pltpu.VMEM((2, 1, t), F32)],
        compiler_params=_params("parallel", "parallel", "arbitrary"),
        name="moba_attn",
    )(qa, ka, va, bias)


HEAD_GROUP = 8
SAMPLE_HEADS_PER_STEP = 8


def _sample_bias_kernel(relt_ref, o_ref, *, past):
    nh, n = o_ref.shape
    bucket = _rel_bucket(past - lax.broadcasted_iota(jnp.int32, (nh, n), 1))
    relt = relt_ref[...]
    acc = jnp.broadcast_to(relt[:, 0:1], (nh, n))
    for k in range(1, NUM_BUCKETS):
        acc = jnp.where(bucket == k, relt[:, k:k + 1], acc)
    o_ref[...] = acc


def sample_bias(rel_bias, past):
    nh = rel_bias.shape[1]
    return pl.pallas_call(
        functools.partial(_sample_bias_kernel, past=past),
        out_shape=jax.ShapeDtypeStruct((nh, past), F32),
        compiler_params=pltpu.CompilerParams(vmem_limit_bytes=VMEM_LIMIT),
        name="sample_bias",
    )(rel_bias.T)


def _sample_scores_kernel(pt_ref, qt_ref, bias_ref, *refs, scale, pages_per_block):
    page_refs, o_ref, g_ref = refs[:-2], refs[-2], refs[-1]
    nh, _, ps = page_refs[0].shape
    t = pages_per_block * ps
    for h in range(nh):
        qc = qt_ref[:, h:h + 1] * scale
        for blk in range(len(page_refs) // pages_per_block):
            tot = None
            for p in range(pages_per_block):
                sp = jnp.sum(qc * page_refs[blk * pages_per_block + p][h], axis=0, keepdims=True)
                lo = blk * t + p * ps
                o_ref[blk, h:h + 1, p * ps:(p + 1) * ps] = sp + bias_ref[h:h + 1, lo:lo + ps]
                tot = sp if tot is None else tot + sp
            g_ref[blk, h:h + 1, :] = tot


def sample_scores(qt, bias, cache_t, layer, page_table, nbc):
    bsz = page_table.shape[0]
    _, _, nh, dh, ps = cache_t.shape
    ppb = MOBA_BLOCK // ps
    assert MOBA_BLOCK == ppb * ps
    bps = math.gcd(nbc, 4)
    page = lambda i: pl.BlockSpec((None, None, nh, dh, ps),
                                  lambda b, j, pt: (layer, pt[b, j * bps * ppb + i], 0, 0, 0))
    return pl.pallas_call(
        functools.partial(_sample_scores_kernel, scale=dh ** -0.5, pages_per_block=ppb),
        grid_spec=pltpu.PrefetchScalarGridSpec(
            num_scalar_prefetch=1,
            grid=(bsz, nbc // bps),
            in_specs=[pl.BlockSpec((None, dh, nh), lambda b, j, pt: (b, 0, 0)),
                      pl.BlockSpec((nh, bps * MOBA_BLOCK), lambda b, j, pt: (0, j))]
                     + [page(i) for i in range(bps * ppb)],
            out_specs=[pl.BlockSpec((None, bps, nh, MOBA_BLOCK), lambda b, j, pt: (b, j, 0, 0)),
                       pl.BlockSpec((None, bps, nh, ps), lambda b, j, pt: (b, j, 0, 0))]),
        out_shape=[jax.ShapeDtypeStruct((bsz, nbc, nh, MOBA_BLOCK), F32),
                   jax.ShapeDtypeStruct((bsz, nbc, nh, ps), F32)],
        compiler_params=_params("parallel", "arbitrary"),
        name="sample_scores",
    )(page_table, qt, bias, *([cache_t] * (bps * ppb)))


def _sample_select_kernel(s_ref, sel_ref, *, k_sel, inv):
    nb, nh, _ = s_ref.shape
    gate = jnp.sum(s_ref[...], axis=-1) * inv
    n_avail = jnp.full((1, nh), nb, jnp.int32)
    _, picks = _top_k_rows(gate, n_avail, k_sel)
    for r, idx in enumerate(picks):
        sel_ref[r:r + 1, :] = idx.astype(jnp.int32)


def sample_select(gsum, k_sel, inv):
    bsz, nb, nh, t = gsum.shape
    scores = gsum
    return pl.pallas_call(
        functools.partial(_sample_select_kernel, k_sel=k_sel, inv=inv),
        grid=(bsz,),
        in_specs=[pl.BlockSpec((None, nb, nh, t), lambda b: (b, 0, 0, 0))],
        out_specs=pl.BlockSpec((None, k_sel, nh), lambda b: (b, 0, 0)),
        out_shape=jax.ShapeDtypeStruct((bsz, k_sel, nh), jnp.int32),
        compiler_params=_params("parallel"),
        name="sample_select",
    )(scores)


def _sample_attn_kernel(pg_ref, sel_ref, rel_ref, q_ref, kn_ref, vn_ref, *refs, nh, k_sel):
    s_ref, v_refs, o_ref = refs[0], refs[1:-1], refs[-1]
    hps = o_ref.shape[0]
    npg = len(v_refs) // hps
    ppb = npg // k_sel
    b = pl.program_id(0)
    h0 = pl.program_id(1) * hps
    dh, ps = v_refs[0].shape
    hl0 = h0 % q_ref.shape[0]
    for j in range(hps):
        h, hl = h0 + j, hl0 + j
        q = q_ref[pl.ds(hl, 1), :] * (dh ** -0.5)
        scores = [s_ref[sel_ref[(b * nh + h) * k_sel + r], pl.ds(h, 1), :] for r in range(k_sel)]
        s_self = jnp.sum(q * kn_ref[pl.ds(hl, 1), :], axis=1, keepdims=True) + rel_ref[0, h]
        mx = s_self
        for s in scores:
            mx = jnp.maximum(mx, jnp.max(s, axis=1, keepdims=True))
        p_self = jnp.exp(s_self - mx)
        den = p_self
        acc = p_self * vn_ref[pl.ds(hl, 1), :]
        for r, s in enumerate(scores):
            p = jnp.exp(s - mx)
            den = den + jnp.sum(p, axis=1, keepdims=True)
            for i in range(ppb):
                p8 = jnp.broadcast_to(p[:, i * ps:(i + 1) * ps], (8, ps))
                acc = acc + _dot_f32(p8, v_refs[j * npg + r * ppb + i][...], _dot_nt)[0:1, :]
        o_ref[j] = acc / den


def sample_attn(pages, sel_flat, rel_bias, q, k_new, v_new, scores, cache_vt, layer, k_sel):
    bsz, nh, dh = q.shape
    ps = cache_vt.shape[4]
    _, nbc, _, t = scores.shape
    ppb = t // ps
    hg = HEAD_GROUP
    hps = SAMPLE_HEADS_PER_STEP
    assert nh % hg == 0 and hg % hps == 0
    npg = ppb * k_sel

    def page(j, i):
        return pl.BlockSpec((None, None, None, dh, ps),
                            lambda b, s, pg, sel: (layer, pg[(b * nh + s * hps + j) * npg + i], s * hps + j, 0, 0))

    new = lambda: pl.BlockSpec((None, hg, dh), lambda b, s, pg, sel: (b, (s * hps) // hg, 0))
    return pl.pallas_call(
        functools.partial(_sample_attn_kernel, nh=nh, k_sel=k_sel),
        grid_spec=pltpu.PrefetchScalarGridSpec(
            num_scalar_prefetch=2,
            grid=(bsz, nh // hps),
            in_specs=[pl.BlockSpec(memory_space=pltpu.SMEM), new(), new(), new(),
                      pl.BlockSpec((None, nbc, nh, t), lambda b, s, pg, sel: (b, 0, 0, 0))]
                     + [page(j, i) for j in range(hps) for i in range(npg)],
            out_specs=pl.BlockSpec((None, hps, 1, dh), lambda b, s, pg, sel: (b, s, 0, 0))),
        out_shape=jax.ShapeDtypeStruct((bsz, nh, 1, dh), F32),
        compiler_params=_params("parallel", "arbitrary"),
        name="sample_attn",
    )(pages, sel_flat, rel_bias, q, k_new, v_new, scores, *([cache_vt] * (hps * npg)))


def _mlstm_layer(hp, hs, batch, st_c, st_n, st_m, g_mix, w_in, b_gates, g_head, w_out):
    nh = ML_HEADS
    d = hp.shape[1]
    vw = w_out.shape[0]
    dv = vw // nh
    dqk = dv // 2
    qkw = nh * dqk
    w_main = jnp.concatenate([w_in[:, :qkw], w_in[:, 2 * qkw:2 * qkw + 2 * vw]], axis=1).astype(BF16)
    wkt = w_in[:, qkw:2 * qkw].T.astype(BF16)
    wgt = w_in[:, 2 * qkw + 2 * vw:].T
    w_out_b = w_out.astype(BF16)

    q, v, o, kt, gt = ml_proj(hp, g_mix, w_main, wkt, wgt, b_gates, BF16)
    hsp, c_p, n_p, m_p = mlstm_prompt(q, v, o, kt, gt, g_head, batch)

    bs = hs.shape[0]
    q, v, o, kt, gt = ml_proj(hs, g_mix, w_main, wkt, wgt, b_gates, F32)
    hss, c_s, n_s, m_s = mlstm_sample(q.reshape(bs, nh, dqk), kt.T.reshape(bs, nh, dqk), v.reshape(bs, nh, dv),
                                      o.reshape(bs, nh, dv), gt.T.reshape(bs, 1, 2 * nh), g_head,
                                      st_c, st_n, st_m.reshape(bs, 1, nh))
    return hsp, hss.reshape(bs, vw), w_out_b, (c_p, n_p, m_p), (c_s, n_s, m_s.reshape(bs, nh))


def _moba_layer(hp, hs, batch, cache_kt, cache_vt, layer, page_table, bias, kv_group, rel_bias, g_mix, w_qkv,
                w_out, kbuf, vbuf):
    _, _, nh, dh, ps = cache_kt.shape
    w = w_qkv.shape[1] // 3
    assert dh == LANES // 2 and w == nh * dh
    w_qkv_b = w_qkv.astype(BF16)
    w_out_b = w_out.astype(BF16)

    q, k, v = norm_proj(hp, g_mix, w_qkv_b, (w, w, w))
    kmeans = block_means(k)
    qa, ka, va, kbuf, vbuf = moba_select(q, k, v, kmeans, batch, layer, kbuf, vbuf, kv_group)
    o = moba_attn(qa, ka, va, bias[0], kv_group, bias[1], bias[2])

    bs, n_pages = page_table.shape
    past = n_pages * ps
    sbias = bias[3]
    assert sbias.shape == (nh, past)
    nbc = past // MOBA_BLOCK
    assert past == nbc * MOBA_BLOCK and nbc >= 1, "a partially filled tail block is not supported"
    k_sel = min(MOBA_TOPK, nbc)
    (qkv_s,) = norm_proj(hs, g_mix, w_qkv_b, (3 * w,))
    q_s, k_s, v_s = (qkv_s[:, i * w:(i + 1) * w].reshape(bs, nh, dh) for i in range(3))
    ppb = MOBA_BLOCK // ps
    scores, gsum = sample_scores(jnp.transpose(q_s, (0, 2, 1)), sbias, cache_kt, layer, page_table, nbc)
    sel = sample_select(gsum, k_sel, 1.0 / (MOBA_BLOCK * dh ** -0.5))
    sel_bhk = jnp.transpose(sel, (0, 2, 1))
    logical = sel_bhk[..., None] * ppb + jnp.arange(ppb, dtype=jnp.int32)
    pages = jnp.take_along_axis(page_table, logical.reshape(bs, -1), axis=1)
    o_s = sample_attn(pages.reshape(-1), sel_bhk.reshape(-1), rel_bias, q_s, k_s, v_s,
                      scores, cache_vt, layer, k_sel)
    return o, o_s.reshape(bs, w), w_out_b, kbuf, vbuf, (k_s, v_s)


def kernel(x_prompt, x_sample, cache_k, cache_v, state_C, state_n, state_m, page_table, norm_mix, norm_ffn,
           norm_final, rel_bias, w_ml_in, b_ml_gates, g_ml_head, w_ml_out, w_attn_qkv, w_attn_out, w_ffn_in,
           w_ffn_out):
    batch, seq, d = x_prompt.shape
    bs, dec_seq, _ = x_sample.shape
    assert dec_seq == 1 and seq % MOBA_BLOCK == 0
    depth = norm_mix.shape[0]
    nh_at = cache_k.shape[3]
    dh = cache_k.shape[4]
    hp = x_prompt.reshape(batch * seq, d)
    hs = x_sample.reshape(bs, d)
    dff = w_ffn_out.shape[1]
    assert dff % FFN_CHUNK == 0
    w_ffn_in_b = jnp.transpose(w_ffn_in.astype(BF16).reshape(depth, d, 2, dff // FFN_CHUNK, FFN_CHUNK),
                               (0, 2, 3, 1, 4))
    w_ffn_out_b = w_ffn_out.astype(BF16).reshape(depth, dff // FFN_CHUNK, FFN_CHUNK, d)
    nb = seq // MOBA_BLOCK
    kv_group = math.gcd(nb, KV_GROUP)
    d_min, d_sat = 1 - kv_group, _saturation_block_distance(MOBA_BLOCK, nb)
    bias = (bias_tiles(rel_bias, d_min, d_sat, MOBA_BLOCK), d_min, d_sat,
            sample_bias(rel_bias, page_table.shape[1] * cache_k.shape[2]))
    cache_kt = jnp.transpose(cache_k, (0, 1, 3, 4, 2))
    cache_vt = jnp.transpose(cache_v, (0, 1, 3, 4, 2))

    kbuf = jnp.zeros((depth // 2, batch, nh_at * dh, seq), F32)
    vbuf = jnp.zeros((depth // 2, batch, nh_at * dh, seq), F32)
    kv_s, st_p, st_s = [], [], []
    for layer in range(depth):
        j = layer // 2
        if layer % 2 == 0:
            ap, asm, w_o, sp, ss = _mlstm_layer(hp, hs, batch, state_C[j], state_n[j], state_m[j], norm_mix[layer],
                                                w_ml_in[j], b_ml_gates[j], g_ml_head[j], w_ml_out[j])
            st_p.append(sp)
            st_s.append(ss)
        else:
            ap, asm, w_o, kbuf, vbuf, ks = _moba_layer(hp, hs, batch, cache_kt, cache_vt, j, page_table, bias,
                                                       kv_group, rel_bias, norm_mix[layer], w_attn_qkv[j],
                                                       w_attn_out[j], kbuf, vbuf)
            kv_s.append(ks)
        g_last = norm_final if layer == depth - 1 else None
        hp = proj_ffn(ap, w_o, hp, norm_ffn[layer], w_ffn_in_b, w_ffn_out_b, layer, g_last)
        hs = proj_ffn(asm, w_o, hs, norm_ffn[layer], w_ffn_in_b, w_ffn_out_b, layer, g_last)

    y_prompt = hp.reshape(batch, seq, d)
    y_sample = hs.reshape(bs, dec_seq, d)
    stack = lambda items, i, shape: jnp.stack([it[i] for it in items]).reshape((len(items),) + shape)
    to_rows = lambda buf: jnp.transpose(buf.reshape(buf.shape[0], batch, nh_at, dh, seq), (0, 1, 4, 2, 3))
    return (y_prompt, y_sample, to_rows(kbuf), to_rows(vbuf),
            stack(kv_s, 0, (bs, dec_seq, nh_at, dh)), stack(kv_s, 1, (bs, dec_seq, nh_at, dh)),
            jnp.stack([s[0] for s in st_p]), jnp.stack([s[1] for s in st_p]), jnp.stack([s[2] for s in st_p]),
            jnp.stack([s[0] for s in st_s]), jnp.stack([s[1] for s in st_s]), jnp.stack([s[2] for s in st_s]))
```

```python
import functools
import math

import jax
import jax.numpy as jnp
from jax import lax
from jax.experimental import pallas as pl
from jax.experimental.pallas import tpu as pltpu

F32 = jnp.float32
BF16 = jnp.bfloat16

EPS = 1e-6
ML_HEADS = 8
ML_CHUNK = 256
MOBA_BLOCK = 256
MOBA_TOPK = 3
KV_GROUP = 4
FFN_CHUNK = 256
SUM_ROWS = 16
LOG2E = 1.4426950408889634
NUM_BUCKETS = 32
MAX_DISTANCE = 4096
NEG_BIG = -1e30
LANES = 128
VMEM_LIMIT = 56 * 1024 * 1024


def _dot(a, b):
    return jnp.dot(a, b, preferred_element_type=F32)


def _dot_nt(a, b):
    return lax.dot_general(a, b, (((1,), (1,)), ((), ())), preferred_element_type=F32)


def _split2(x):
    hi = x.astype(BF16)
    lo = (x - hi.astype(F32)).astype(BF16)
    return hi, lo


def _split3(x):
    hi = x.astype(BF16)
    r = x - hi.astype(F32)
    mid = r.astype(BF16)
    lo = (r - mid.astype(F32)).astype(BF16)
    return hi, mid, lo


def _dot_f32(a, b, dot=_dot):
    ah, al = _split2(a)
    bh, bl = _split2(b)
    return dot(ah, bh) + dot(ah, bl) + dot(al, bh)


def _norm_rows(x, g):
    return x * lax.rsqrt(jnp.mean(x * x, axis=-1, keepdims=True) + EPS) * g


def _log_sigmoid(x):
    return jnp.minimum(x, 0.0) - jnp.log1p(jnp.exp(-jnp.abs(x)))


def _rel_bucket(dist):
    n = jnp.maximum(dist, 0)
    max_exact = NUM_BUCKETS // 2
    nf = jnp.maximum(n, 1).astype(F32)
    large = max_exact + (jnp.log(nf / max_exact) / math.log(MAX_DISTANCE / max_exact)
                         * (NUM_BUCKETS - max_exact)).astype(jnp.int32)
    return jnp.where(n < max_exact, n, jnp.minimum(large, NUM_BUCKETS - 1))


def _bias_lookup(bucket, rel_ref, h):
    acc = jnp.full(bucket.shape, rel_ref[0, h], F32)
    for k in range(1, NUM_BUCKETS):
        acc = jnp.where(bucket == k, rel_ref[k, h], acc)
    return acc


def _params(*sem):
    return pltpu.CompilerParams(dimension_semantics=sem, vmem_limit_bytes=VMEM_LIMIT)


def _row_tile(m, target):
    t = min(m, target)
    assert m % t == 0, (m, t)
    return t


def _proj_kernel(x_ref, g_ref, w_ref, *o_refs):
    xb = _norm_rows(x_ref[...], g_ref[...]).astype(BF16)
    c0 = 0
    for o_ref in o_refs:
        n = o_ref.shape[1]
        for c in range(0, n, 512):
            w = min(512, n - c)
            o_ref[:, c:c + w] = _dot(xb, w_ref[:, c0 + c:c0 + c + w])
        c0 += n


def norm_proj(x, g, w, widths, tm=512):
    m, d = x.shape
    tm = _row_tile(m, tm)
    n = w.shape[1]
    assert sum(widths) == n
    return pl.pallas_call(
        _proj_kernel,
        grid=(m // tm,),
        in_specs=[pl.BlockSpec((tm, d), lambda i: (i, 0)),
                  pl.BlockSpec((1, d), lambda i: (0, 0)),
                  pl.BlockSpec((d, n), lambda i: (0, 0))],
        out_specs=[pl.BlockSpec((tm, wd), lambda i: (i, 0)) for wd in widths],
        out_shape=[jax.ShapeDtypeStruct((m, wd), F32) for wd in widths],
        compiler_params=_params("parallel"),
        name="norm_proj",
    )(x, g.reshape(1, d), w)


def _ml_proj_kernel(x_ref, g_ref, w_ref, wkt_ref, wgt_ref, bg_ref, q_ref, v_ref, o_ref, kt_ref, gt_ref):
    xn = _norm_rows(x_ref[...], g_ref[...])
    xb = xn.astype(BF16)
    xlo = (xn - xb.astype(F32)).astype(BF16)
    c0 = 0
    for out in (q_ref, v_ref, o_ref):
        n = out.shape[1]
        for c in range(0, n, 512):
            out[:, c:c + 512] = _dot(xb, w_ref[:, c0 + c:c0 + c + 512]).astype(out.dtype)
        c0 += n
    kt_ref[...] = _dot_nt(wkt_ref[...], xb)
    gh, gl = _split2(wgt_ref[...])
    gt_ref[...] = _dot_nt(gh, xb) + _dot_nt(gh, xlo) + _dot_nt(gl, xb) + bg_ref[...]


def ml_proj(x, g, w_main, wkt, wgt, bg, row_dtype, tm=512):
    m, d = x.shape
    tm = _row_tile(m, tm)
    qk = wkt.shape[0]
    vw = (w_main.shape[1] - qk) // 2
    ng = wgt.shape[0]
    return pl.pallas_call(
        _ml_proj_kernel,
        grid=(m // tm,),
        in_specs=[pl.BlockSpec((tm, d), lambda i: (i, 0)),
                  pl.BlockSpec((1, d), lambda i: (0, 0)),
                  pl.BlockSpec(w_main.shape, lambda i: (0, 0)),
                  pl.BlockSpec(wkt.shape, lambda i: (0, 0)),
                  pl.BlockSpec(wgt.shape, lambda i: (0, 0)),
                  pl.BlockSpec((ng, 1), lambda i: (0, 0))],
        out_specs=[pl.BlockSpec((tm, qk), lambda i: (i, 0)),
                   pl.BlockSpec((tm, vw), lambda i: (i, 0)),
                   pl.BlockSpec((tm, vw), lambda i: (i, 0)),
                   pl.BlockSpec((qk, tm), lambda i: (0, i)),
                   pl.BlockSpec((ng, tm), lambda i: (0, i))],
        out_shape=[jax.ShapeDtypeStruct((m, qk), row_dtype),
                   jax.ShapeDtypeStruct((m, vw), row_dtype),
                   jax.ShapeDtypeStruct((m, vw), row_dtype),
                   jax.ShapeDtypeStruct((qk, m), F32),
                   jax.ShapeDtypeStruct((ng, m), F32)],
        compiler_params=_params("parallel"),
        name="ml_proj",
    )(x, g.reshape(1, d), w_main, wkt, wgt, bg.reshape(ng, 1))


def _proj_ffn_kernel(a_ref, wm_ref, r_ref, g_ref, win_ref, wo_ref, *rest):
    gf_ref = rest[0] if len(rest) == 4 else None
    o_ref, xn_s, acc_s = rest[-3:]
    h = r_ref[...] + _dot(a_ref[...].astype(BF16), wm_ref[...])
    xn_s[...] = _norm_rows(h, g_ref[...]).astype(BF16)
    acc_s[...] = h

    def chunk(f, carry):
        xb = xn_s[...]
        gp = _dot(xb, win_ref[0, f])
        up = _dot(xb, win_ref[1, f])
        act = (gp * jax.nn.sigmoid(gp) * up).astype(BF16)
        acc_s[...] += _dot(act, wo_ref[f])
        return carry

    lax.fori_loop(0, wo_ref.shape[0], chunk, 0, unroll=True)
    out = acc_s[...]
    o_ref[...] = out if gf_ref is None else _norm_rows(out, gf_ref[...])


def proj_ffn(a, w_mix, res, g, w_in, w_out, layer, g_final=None, tm=1024):
    m, k = a.shape
    d = res.shape[1]
    tm = _row_tile(m, tm)
    once = pl.Buffered(1)
    row = lambda: pl.BlockSpec((1, d), lambda i: (0, 0))
    in_specs = [pl.BlockSpec((tm, k), lambda i: (i, 0)),
                pl.BlockSpec((k, d), lambda i: (0, 0), pipeline_mode=once),
                pl.BlockSpec((tm, d), lambda i: (i, 0)),
                row(),
                pl.BlockSpec((None,) + w_in.shape[1:], lambda i: (layer, 0, 0, 0, 0), pipeline_mode=once),
                pl.BlockSpec((None,) + w_out.shape[1:], lambda i: (layer, 0, 0, 0), pipeline_mode=once)]
    args = [a, w_mix, res, g.reshape(1, d), w_in, w_out]
    if g_final is not None:
        in_specs.append(row())
        args.append(g_final.reshape(1, d))
    return pl.pallas_call(
        _proj_ffn_kernel,
        grid=(m // tm,),
        in_specs=in_specs,
        out_specs=pl.BlockSpec((tm, d), lambda i: (i, 0)),
        out_shape=jax.ShapeDtypeStruct((m, d), F32),
        scratch_shapes=[pltpu.VMEM((tm, d), BF16), pltpu.VMEM((tm, d), F32)],
        compiler_params=_params("parallel"),
        name="proj_ffn",
    )(*args)


def _mlstm_chunk_kernel(q_ref, v_ref, o_ref, kt_ref, gt_ref, gh_ref, hs_ref, caug_ref, m_ref, caug_s, m_s):
    c = pl.program_id(1)
    L = q_ref.shape[0]
    nh = caug_s.shape[0]
    dqk = caug_s.shape[1]
    dv = caug_s.shape[2] // 2

    @pl.when(c == 0)
    def _():
        caug_s[...] = jnp.zeros_like(caug_s)
        m_s[...] = jnp.zeros_like(m_s)

    row = lax.broadcasted_iota(jnp.int32, (L, L), 0)
    col = lax.broadcasted_iota(jnp.int32, (L, L), 1)
    causal = col <= row
    lower = jnp.where(causal, 1.0, 0.0).astype(BF16)
    upper = jnp.where(row <= col, 1.0, 0.0).astype(BF16)
    gt = gt_ref[...]
    ig = gt[0:nh, :]
    logf = _log_sigmoid(gt[nh:2 * nh, :])
    parts = _split3(logf)
    f_row = _dot(parts[0], upper) + _dot(parts[1], upper) + _dot(parts[2], upper)
    f_col = _dot_nt(lower, parts[0]) + _dot_nt(lower, parts[1]) + _dot_nt(lower, parts[2])
    a_rows = ig - f_row
    ones_col = jnp.where(lax.broadcasted_iota(jnp.int32, (L, dv), 1) == 0, 1.0, 0.0).astype(BF16)

    for h in range(nh):
        m_prev = m_s[h:h + 1, 0:1]
        a_row = a_rows[h:h + 1, :]
        amat = jnp.where(causal, a_row, -jnp.inf)
        g = jnp.maximum(jnp.max(amat, axis=1, keepdims=True), m_prev)
        w = jnp.exp(amat - g)
        qh = q_ref[:, h * dqk:(h + 1) * dqk].astype(BF16)
        kth = kt_ref[h * dqk:(h + 1) * dqk, :] * (dqk ** -0.5)
        p = (_dot(qh, kth.astype(BF16)) * w).astype(BF16)
        vaug = jnp.concatenate([v_ref[:, h * dv:(h + 1) * dv].astype(BF16), ones_col], axis=1)
        caug = caug_s[h]
        a_int = jnp.exp(m_prev - g)
        tot = a_int * _dot(qh, caug.astype(BF16)) + _dot(p, vaug)
        m_t = f_col[:, h:h + 1] + g
        hh = tot[:, :dv] / jnp.maximum(jnp.abs(tot[:, dv:dv + 1]), jnp.exp(-m_t))
        hn = hh * lax.rsqrt(jnp.mean(hh * hh, axis=-1, keepdims=True) + EPS) * gh_ref[:, h * dv:(h + 1) * dv]
        hs_ref[:, h * dv:(h + 1) * dv] = (hn * jax.nn.sigmoid(o_ref[:, h * dv:(h + 1) * dv].astype(F32))).astype(hs_ref.dtype)
        g_last = g[L - 1:L, :]
        w_last = jnp.exp(a_row - g_last)
        caug_s[h] = jnp.exp(m_prev - g_last) * caug + _dot((kth * w_last).astype(BF16), vaug)
        m_s[h:h + 1, :] = jnp.broadcast_to(f_row[h:h + 1, L - 1:L] + g_last, (1, m_s.shape[1]))

    @pl.when(c == pl.num_programs(1) - 1)
    def _():
        caug_ref[...] = caug_s[...]
        m_ref[...] = m_s[...]


def mlstm_prompt(q, v, o, kt, gt, g_head, batch):
    m, qkw = q.shape
    vw = v.shape[1]
    nh = gt.shape[0] // 2
    dqk, dv = qkw // nh, vw // nh
    s = m // batch
    L = math.gcd(s, ML_CHUNK)
    nc = s // L
    hs, caug, mm = pl.pallas_call(
        _mlstm_chunk_kernel,
        grid=(batch, nc),
        in_specs=[pl.BlockSpec((L, qkw), lambda b, c: (b * nc + c, 0)),
                  pl.BlockSpec((L, vw), lambda b, c: (b * nc + c, 0)),
                  pl.BlockSpec((L, vw), lambda b, c: (b * nc + c, 0)),
                  pl.BlockSpec((qkw, L), lambda b, c: (0, b * nc + c)),
                  pl.BlockSpec((2 * nh, L), lambda b, c: (0, b * nc + c)),
                  pl.BlockSpec((1, vw), lambda b, c: (0, 0))],
        out_specs=[pl.BlockSpec((L, vw), lambda b, c: (b * nc + c, 0)),
                   pl.BlockSpec((None, nh, dqk, 2 * dv), lambda b, c: (b, 0, 0, 0)),
                   pl.BlockSpec((None, nh, LANES), lambda b, c: (b, 0, 0))],
        out_shape=[jax.ShapeDtypeStruct((m, vw), BF16),
                   jax.ShapeDtypeStruct((batch, nh, dqk, 2 * dv), F32),
                   jax.ShapeDtypeStruct((batch, nh, LANES), F32)],
        scratch_shapes=[pltpu.VMEM((nh, dqk, 2 * dv), F32), pltpu.VMEM((nh, LANES), F32)],
        compiler_params=_params("parallel", "arbitrary"),
        name="mlstm_prompt",
    )(q, v, o, kt, gt, g_head.reshape(1, vw))
    return hs, caug[..., :dv], caug[..., dv], mm[..., 0]


def _mlstm_step_kernel(q_ref, k_ref, v_ref, o_ref, gt_ref, m_ref, gh_ref, c_ref, n_ref,
                       hs_ref, cn_ref, nn_ref, mn_ref):
    nh, dqk, dv = c_ref.shape
    scale = dqk ** -0.5
    q = q_ref[...]
    k = k_ref[...] * scale
    eye = jnp.where(lax.broadcasted_iota(jnp.int32, (dqk, dqk), 0)
                    == lax.broadcasted_iota(jnp.int32, (dqk, dqk), 1), 1.0, 0.0).astype(BF16)
    q3, k3 = _split3(q), _split3(k)
    q_cols = _dot_nt(eye, q3[0]) + _dot_nt(eye, q3[1]) + _dot_nt(eye, q3[2])
    k_cols = _dot_nt(eye, k3[0]) + _dot_nt(eye, k3[1]) + _dot_nt(eye, k3[2])
    gt = gt_ref[...]
    for h in range(nh):
        ig = gt[:, h:h + 1]
        logf = _log_sigmoid(gt[:, nh + h:nh + h + 1])
        m_inter = m_ref[:, h:h + 1] + logf
        m_t = jnp.maximum(m_inter, ig)
        w = jnp.exp(ig - m_t)
        a = jnp.exp(m_inter - m_t)
        q_row, k_row = q[h:h + 1, :], k[h:h + 1, :]
        q_col, k_col = q_cols[:, h:h + 1], k_cols[:, h:h + 1]
        v_row = v_ref[h:h + 1, :]
        c_h = c_ref[h]
        n_row = n_ref[h:h + 1, :]
        sqk = jnp.sum(q_row * k_row, axis=1, keepdims=True) * w
        num = a * jnp.sum(q_col * c_h, axis=0, keepdims=True) + sqk * v_row
        den = a * jnp.sum(q_row * n_row, axis=1, keepdims=True) + sqk
        hh = num / jnp.maximum(jnp.abs(den), jnp.exp(-m_t))
        hn = hh * lax.rsqrt(jnp.mean(hh * hh, axis=-1, keepdims=True) + EPS) * gh_ref[h:h + 1, :]
        hs_ref[h:h + 1, :] = (hn * jax.nn.sigmoid(o_ref[h:h + 1, :])).astype(hs_ref.dtype)
        cn_ref[h] = a * c_h + (w * k_col) * v_row
        nn_ref[h:h + 1, :] = a * n_row + w * k_row
        mn_ref[:, h:h + 1] = m_t


def mlstm_sample(q, k, v, o, gates, g_head, c0, n0, m0):
    bsz, nh, dqk, dv = c0.shape
    row3 = lambda w: pl.BlockSpec((None, nh, w), lambda b: (b, 0, 0))
    return pl.pallas_call(
        _mlstm_step_kernel,
        grid=(bsz,),
        in_specs=[row3(dqk), row3(dqk), row3(dv), row3(dv),
                  pl.BlockSpec((None, 1, 2 * nh), lambda b: (b, 0, 0)),
                  pl.BlockSpec((None, 1, nh), lambda b: (b, 0, 0)),
                  pl.BlockSpec((nh, dv), lambda b: (0, 0)),
                  pl.BlockSpec((None, nh, dqk, dv), lambda b: (b, 0, 0, 0)),
                  row3(dqk)],
        out_specs=[row3(dv),
                   pl.BlockSpec((None, nh, dqk, dv), lambda b: (b, 0, 0, 0)),
                   row3(dqk),
                   pl.BlockSpec((None, 1, nh), lambda b: (b, 0, 0))],
        out_shape=[jax.ShapeDtypeStruct((bsz, nh, dv), F32),
                   jax.ShapeDtypeStruct((bsz, nh, dqk, dv), F32),
                   jax.ShapeDtypeStruct((bsz, nh, dqk), F32),
                   jax.ShapeDtypeStruct((bsz, 1, nh), F32)],
        compiler_params=_params("parallel"),
        name="mlstm_sample",
    )(q, k, v, o, gates, m0, g_head.reshape(nh, dv), c0, n0)


def _bucket_bounds(lo, hi):
    def bucket(n):
        n = max(n, 0)
        max_exact = NUM_BUCKETS // 2
        if n < max_exact:
            return n
        return min(max_exact + int(math.log(n / max_exact) / math.log(MAX_DISTANCE / max_exact)
                                   * (NUM_BUCKETS - max_exact)), NUM_BUCKETS - 1)
    return max(bucket(lo) - 1, 0), min(bucket(hi) + 1, NUM_BUCKETS - 1)


def _bias_tiles_kernel(kr_ref, rel_ref, o_ref, *, d_min):
    step = pl.program_id(0)
    d = step + d_min
    nh, t, _ = o_ref.shape
    dist = d * t + lax.broadcasted_iota(jnp.int32, (t, t), 1) - lax.broadcasted_iota(jnp.int32, (t, t), 0)
    bucket = _rel_bucket(dist)
    k_lo, k_hi = kr_ref[2 * step], kr_ref[2 * step + 1]
    for h in range(nh):
        acc = lax.fori_loop(k_lo + 1, k_hi + 1, lambda k, a: jnp.where(bucket == k, rel_ref[k, h], a),
                            jnp.full((t, t), rel_ref[k_lo, h], F32))
        o_ref[h] = jnp.where(dist >= 0, acc * LOG2E, NEG_BIG)


def bias_tiles(rel_bias, d_min, d_max, t):
    nh = rel_bias.shape[1]
    nd = d_max - d_min + 1
    bounds = [b for d in range(d_min, d_max + 1) for b in _bucket_bounds(d * t - (t - 1), d * t + (t - 1))]
    return pl.pallas_call(
        functools.partial(_bias_tiles_kernel, d_min=d_min),
        grid_spec=pltpu.PrefetchScalarGridSpec(
            num_scalar_prefetch=1,
            grid=(nd,),
            in_specs=[pl.BlockSpec(memory_space=pltpu.SMEM)],
            out_specs=pl.BlockSpec((nh, None, t, t), lambda d, kr: (0, d, 0, 0))),
        out_shape=jax.ShapeDtypeStruct((nh, nd, t, t), F32),
        compiler_params=_params("parallel"),
        name="bias_tiles",
    )(jnp.asarray(bounds, jnp.int32), rel_bias)


def _kmeans_kernel(k_ref, o_ref):
    nblk = o_ref.shape[0]
    t = k_ref.shape[0] // nblk
    for j in range(nblk):
        o_ref[j:j + 1, :] = jnp.sum(k_ref[j * t:(j + 1) * t, :], axis=0, keepdims=True) * (1.0 / t)


def block_means(k, blocks_per_step=8):
    m, w = k.shape
    rows = MOBA_BLOCK * blocks_per_step
    assert m % rows == 0
    return pl.pallas_call(
        _kmeans_kernel,
        grid=(m // rows,),
        in_specs=[pl.BlockSpec((rows, w), lambda i: (i, 0))],
        out_specs=pl.BlockSpec((blocks_per_step, w), lambda i: (i, 0)),
        out_shape=jax.ShapeDtypeStruct((m // MOBA_BLOCK, w), F32),
        compiler_params=_params("parallel"),
        name="block_means",
    )(k)


def _top_k_rows(gate, n_avail, k_sel):
    nb = gate.shape[0]
    jidx = lax.broadcasted_iota(jnp.int32, gate.shape, 0).astype(F32)
    selected = jnp.zeros(gate.shape, F32)
    picks = []
    for r in range(k_sel):
        mx = jnp.max(gate, axis=0, keepdims=True)
        idx = jnp.min(jnp.where(gate == mx, jidx, float(nb)), axis=0, keepdims=True)
        hit = jidx == idx
        selected = jnp.where(jnp.logical_and(hit, r < n_avail), 1.0, selected)
        gate = jnp.where(hit, -jnp.inf, gate)
        picks.append(idx)
    return selected > 0.5, picks


def _moba_select_kernel(q_ref, k_ref, v_ref, km_ref, kbuf_ref, vbuf_ref, qa_ref, ka_ref, va_ref, kt_ref, vt_ref,
                        *, nb, k_sel):
    del kbuf_ref, vbuf_ref
    t = pl.program_id(2)
    tq = q_ref.shape[0]
    half = LANES // 2
    shift = MOBA_BLOCK.bit_length() - 1
    q2, k2 = q_ref[...], k_ref[...]
    qt = q2.T * (half ** -0.5 * LOG2E)
    vt = v_ref[...].T
    kt_ref[...] = k2.T
    vt_ref[...] = vt
    ngrp, _, gw = va_ref.shape
    va = jnp.concatenate([vt.astype(BF16), jnp.ones((SUM_ROWS, tq), BF16)], axis=0)
    for gi in range(ngrp):
        va_ref[gi] = va[:, gi * gw:(gi + 1) * gw]
    km = km_ref[...]
    lane = lax.broadcasted_iota(jnp.int32, (tq, LANES), 1)
    rowblk = (t * tq + lax.broadcasted_iota(jnp.int32, (tq, LANES), 0)) >> shift
    feat = lax.broadcasted_iota(jnp.int32, (LANES, tq), 0)
    km_lane = lax.broadcasted_iota(jnp.int32, km.shape, 1)
    pos = t * tq + lax.broadcasted_iota(jnp.int32, (1, tq), 1)
    n_avail = jnp.minimum(pos >> shift, nb)
    jidx = lax.broadcasted_iota(jnp.int32, (nb, tq), 0)
    for hh in range(2):
        off = 0 if hh else half
        kmh = jnp.where((km_lane >= half) if hh else (km_lane < half), km, 0.0)
        gate = _dot_f32(kmh, q2, _dot_nt)
        gate = jnp.where(jidx < n_avail, gate, -jnp.inf)
        selected, _ = _top_k_rows(gate, n_avail, k_sel)
        code = jnp.where(jnp.logical_or(selected, jidx == (pos >> shift)), 0.0, NEG_BIG)
        pieces = [code, jnp.zeros((LANES - nb - off, tq), F32)]
        if off:
            pieces = [jnp.zeros((off, tq), F32)] + pieces
        code_rows = jnp.concatenate(pieces, axis=0)
        qa_ref[hh] = jnp.where((feat >= half) if hh else (feat < half), qt, code_rows).astype(BF16)
        onehot = jnp.where((lane - off) == rowblk, 1.0, 0.0)
        ka_ref[hh] = jnp.where((lane >= half) if hh else (lane < half), k2, onehot).astype(BF16)


def moba_select(q, k, v, kmeans, batch, slot, kbuf, vbuf, kv_group, tq=2048):
    m, w = q.shape
    s = m // batch
    nb = s // MOBA_BLOCK
    npair = w // LANES
    gw = kv_group * MOBA_BLOCK
    tq = max(min(tq, s), gw)
    nt = s // tq
    assert nb <= LANES // 2 and s % tq == 0 and tq % gw == 0 and kbuf.shape[1:] == (batch, w, s)
    k_sel = min(MOBA_TOPK, nb)
    rows = lambda: pl.BlockSpec((tq, LANES), lambda b, p, t: (b * nt + t, p))
    feat = lambda: pl.BlockSpec((None, None, LANES, tq), lambda b, p, t: (slot, b, p, t))
    return pl.pallas_call(
        functools.partial(_moba_select_kernel, nb=nb, k_sel=k_sel),
        grid=(batch, npair, nt),
        in_specs=[rows(), rows(), rows(),
                  pl.BlockSpec((nb, LANES), lambda b, p, t: (b, p)),
                  pl.BlockSpec(memory_space=pl.ANY), pl.BlockSpec(memory_space=pl.ANY)],
        out_specs=[pl.BlockSpec((None, 2, LANES, tq), lambda b, p, t: (b, p, 0, t)),
                   pl.BlockSpec((None, 2, tq, LANES), lambda b, p, t: (b, p, t, 0)),
                   pl.BlockSpec((None, None, tq // gw, LANES + SUM_ROWS, gw), lambda b, p, t: (b, p, t, 0, 0)),
                   feat(), feat()],
        out_shape=[jax.ShapeDtypeStruct((batch, 2 * npair, LANES, s), BF16),
                   jax.ShapeDtypeStruct((batch, 2 * npair, s, LANES), BF16),
                   jax.ShapeDtypeStruct((batch, npair, s // gw, LANES + SUM_ROWS, gw), BF16),
                   jax.ShapeDtypeStruct(kbuf.shape, F32),
                   jax.ShapeDtypeStruct(vbuf.shape, F32)],
        input_output_aliases={4: 3, 5: 4},
        compiler_params=_params("parallel", "parallel", "parallel"),
        name="moba_select",
    )(q, k, v, kmeans, kbuf, vbuf)


def _moba_attn_kernel(qa_ref, ka_ref, va_ref, bias_ref, o_ref, s_a, s_b, m_a, m_b, *, kv_group, d_min, d_sat):
    s = pl.program_id(2)
    nb = pl.num_programs(2) - 1
    t = qa_ref.shape[2]
    gw = kv_group * t
    half = LANES // 2
    n_cur = jnp.where(s < nb, s // kv_group + 1, 0)
    n_prev = jnp.where(s > 0, (s + kv_group - 1) // kv_group, 0)
    n_both = jnp.minimum(n_cur, n_prev)
    qs = (qa_ref[0], qa_ref[1])

    def fold(x, op):
        return op(x.reshape(x.shape[0] // 8, 8, x.shape[1]), axis=0)

    def run(s_cur, m_cur, s_prev, m_prev):
        m_old = (m_prev[0], m_prev[1])

        def scores(g, rmax):
            start = pl.multiple_of(g * gw, gw)
            rmax = list(rmax)
            for hh in range(2):
                sc = _dot(ka_ref[hh, pl.ds(start, gw), :], qs[hh])
                for jj in range(kv_group):
                    d = jnp.minimum(s - (g * kv_group + jj), d_sat)
                    sj = sc[jj * t:(jj + 1) * t, :] + bias_ref[hh, d - d_min]
                    s_cur[hh, pl.ds(pl.multiple_of(start + jj * t, t), t), :] = sj
                    rmax[hh] = jnp.maximum(rmax[hh], fold(sj, jnp.max))
            return tuple(rmax)

        def values(g, accs):
            start = pl.multiple_of(g * gw, gw)
            vgrp = va_ref[g]
            out = []
            for hh in range(2):
                p = jnp.exp2((s_prev[hh, pl.ds(start, gw), :] - m_old[hh]).astype(BF16))
                out.append(accs[hh] + _dot(vgrp, p))
            return tuple(out)

        carry = (jnp.full((8, t), NEG_BIG, F32),) * 2 + (jnp.zeros((LANES + SUM_ROWS, t), F32),) * 2
        carry = lax.fori_loop(0, n_both, lambda g, c: scores(g, c[:2]) + values(g, c[2:]), carry)
        rmax = lax.fori_loop(n_both, n_cur, scores, carry[:2])
        acc0, acc1 = lax.fori_loop(n_both, n_prev, values, carry[2:])
        for hh in range(2):
            m_cur[hh] = jnp.max(rmax[hh], axis=0, keepdims=True)

        @pl.when(s > 0)
        def _():
            feat = lax.broadcasted_iota(jnp.int32, (LANES, t), 0)
            out_t = jnp.where(feat < half, acc0[:LANES] / acc0[LANES:LANES + 1],
                              acc1[:LANES] / acc1[LANES:LANES + 1])
            o_ref[...] = out_t.T.astype(o_ref.dtype)

    @pl.when(s % 2 == 0)
    def _():
        run(s_a, m_a, s_b, m_b)

    @pl.when(s % 2 == 1)
    def _():
        run(s_b, m_b, s_a, m_a)


def _saturation_block_distance(t, nb):
    max_exact = NUM_BUCKETS // 2
    n_sat = max_exact * (MAX_DISTANCE / max_exact) ** ((NUM_BUCKETS - 1 - max_exact) / (NUM_BUCKETS - max_exact))
    return min(math.ceil((1.02 * n_sat + t - 1) / t), nb - 1)


def moba_attn(qa, ka, va, bias, kv_group, d_min, d_sat):
    batch, nh, s, _ = ka.shape
    npair = nh // 2
    t = MOBA_BLOCK
    nb = s // t
    assert nb % kv_group == 0 and d_min <= 1 - kv_group and bias.shape[1] == d_sat - d_min + 1
    last = nb - 1
    return pl.pallas_call(
        functools.partial(_moba_attn_kernel, kv_group=kv_group, d_min=d_min, d_sat=d_sat),
        grid=(npair, batch, nb + 1),
        in_specs=[pl.BlockSpec((None, 2, LANES, t), lambda p, b, i: (b, p, 0, jnp.minimum(i, last))),
                  pl.BlockSpec((None, 2, s, LANES), lambda p, b, i: (b, p, 0, 0)),
                  pl.BlockSpec((None, None) + va.shape[2:], lambda p, b, i: (b, p, 0, 0, 0),
                               pipeline_mode=pl.Buffered(1)),
                  pl.BlockSpec((2,) + bias.shape[1:], lambda p, b, i: (p, 0, 0, 0),
                               pipeline_mode=pl.Buffered(1))],
        out_specs=pl.BlockSpec((t, LANES), lambda p, b, i: (b * nb + jnp.maximum(i - 1, 0), p)),
        out_shape=jax.ShapeDtypeStruct((batch * s, npair * LANES), BF16),
        scratch_shapes=[pltpu.VMEM((2, s, t), F32), pltpu.VMEM((2, s, t), F32),
                        pltpu.VMEM((2, 1, t), F32), pltpu.VMEM((2, 1, t), F32)],
        compiler_params=_params("parallel", "parallel", "arbitrary"),
        name="moba_attn",
    )(qa, ka, va, bias)


HEAD_GROUP = 8
SAMPLE_HEADS_PER_STEP = 8


def _sample_bias_kernel(relt_ref, o_ref, *, past):
    nh, n = o_ref.shape
    bucket = _rel_bucket(past - lax.broadcasted_iota(jnp.int32, (nh, n), 1))
    relt = relt_ref[...]
    acc = jnp.broadcast_to(relt[:, 0:1], (nh, n))
    for k in range(1, NUM_BUCKETS):
        acc = jnp.where(bucket == k, relt[:, k:k + 1], acc)
    o_ref[...] = acc


def sample_bias(rel_bias, past):
    nh = rel_bias.shape[1]
    return pl.pallas_call(
        functools.partial(_sample_bias_kernel, past=past),
        out_shape=jax.ShapeDtypeStruct((nh, past), F32),
        compiler_params=pltpu.CompilerParams(vmem_limit_bytes=VMEM_LIMIT),
        name="sample_bias",
    )(rel_bias.T)


def _sample_scores_kernel(pt_ref, qt_ref, bias_ref, *refs, scale, pages_per_block):
    page_refs, o_ref, g_ref = refs[:-2], refs[-2], refs[-1]
    nh, _, ps = page_refs[0].shape
    t = pages_per_block * ps
    for h in range(nh):
        qc = qt_ref[:, h:h + 1] * scale
        for blk in range(len(page_refs) // pages_per_block):
            tot = None
            for p in range(pages_per_block):
                sp = jnp.sum(qc * page_refs[blk * pages_per_block + p][h], axis=0, keepdims=True)
                lo = blk * t + p * ps
                o_ref[blk, h:h + 1, p * ps:(p + 1) * ps] = sp + bias_ref[h:h + 1, lo:lo + ps]
                tot = sp if tot is None else tot + sp
            g_ref[blk, h:h + 1, :] = tot


def sample_scores(qt, bias, cache_t, layer, page_table, nbc):
    bsz = page_table.shape[0]
    _, _, nh, dh, ps = cache_t.shape
    ppb = MOBA_BLOCK // ps
    assert MOBA_BLOCK == ppb * ps
    bps = math.gcd(nbc, 4)
    page = lambda i: pl.BlockSpec((None, None, nh, dh, ps),
                                  lambda b, j, pt: (layer, pt[b, j * bps * ppb + i], 0, 0, 0))
    return pl.pallas_call(
        functools.partial(_sample_scores_kernel, scale=dh ** -0.5, pages_per_block=ppb),
        grid_spec=pltpu.PrefetchScalarGridSpec(
            num_scalar_prefetch=1,
            grid=(bsz, nbc // bps),
            in_specs=[pl.BlockSpec((None, dh, nh), lambda b, j, pt: (b, 0, 0)),
                      pl.BlockSpec((nh, bps * MOBA_BLOCK), lambda b, j, pt: (0, j))]
                     + [page(i) for i in range(bps * ppb)],
            out_specs=[pl.BlockSpec((None, bps, nh, MOBA_BLOCK), lambda b, j, pt: (b, j, 0, 0)),
                       pl.BlockSpec((None, bps, nh, ps), lambda b, j, pt: (b, j, 0, 0))]),
        out_shape=[jax.ShapeDtypeStruct((bsz, nbc, nh, MOBA_BLOCK), F32),
                   jax.ShapeDtypeStruct((bsz, nbc, nh, ps), F32)],
        compiler_params=_params("parallel", "arbitrary"),
        name="sample_scores",
    )(page_table, qt, bias, *([cache_t] * (bps * ppb)))


def _sample_select_kernel(s_ref, sel_ref, *, k_sel, inv):
    nb, nh, _ = s_ref.shape
    gate = jnp.sum(s_ref[...], axis=-1) * inv
    n_avail = jnp.full((1, nh), nb, jnp.int32)
    _, picks = _top_k_rows(gate, n_avail, k_sel)
    for r, idx in enumerate(picks):
        sel_ref[r:r + 1, :] = idx.astype(jnp.int32)


def sample_select(gsum, k_sel, inv):
    bsz, nb, nh, t = gsum.shape
    scores = gsum
    return pl.pallas_call(
        functools.partial(_sample_select_kernel, k_sel=k_sel, inv=inv),
        grid=(bsz,),
        in_specs=[pl.BlockSpec((None, nb, nh, t), lambda b: (b, 0, 0, 0))],
        out_specs=pl.BlockSpec((None, k_sel, nh), lambda b: (b, 0, 0)),
        out_shape=jax.ShapeDtypeStruct((bsz, k_sel, nh), jnp.int32),
        compiler_params=_params("parallel"),
        name="sample_select",
    )(scores)


def _sample_attn_kernel(pg_ref, sel_ref, rel_ref, q_ref, kn_ref, vn_ref, *refs, nh, k_sel):
    s_ref, v_refs, o_ref = refs[0], refs[1:-1], refs[-1]
    hps = o_ref.shape[0]
    npg = len(v_refs) // hps
    ppb = npg // k_sel
    b = pl.program_id(0)
    h0 = pl.program_id(1) * hps
    dh, ps = v_refs[0].shape
    hl0 = h0 % q_ref.shape[0]
    for j in range(hps):
        h, hl = h0 + j, hl0 + j
        q = q_ref[pl.ds(hl, 1), :] * (dh ** -0.5)
        scores = [s_ref[sel_ref[(b * nh + h) * k_sel + r], pl.ds(h, 1), :] for r in range(k_sel)]
        s_self = jnp.sum(q * kn_ref[pl.ds(hl, 1), :], axis=1, keepdims=True) + rel_ref[0, h]
        mx = s_self
        for s in scores:
            mx = jnp.maximum(mx, jnp.max(s, axis=1, keepdims=True))
        p_self = jnp.exp(s_self - mx)
        den = p_self
        acc = p_self * vn_ref[pl.ds(hl, 1), :]
        for r, s in enumerate(scores):
            p = jnp.exp(s - mx)
            den = den + jnp.sum(p, axis=1, keepdims=True)
            for i in range(ppb):
                p8 = jnp.broadcast_to(p[:, i * ps:(i + 1) * ps], (8, ps))
                acc = acc + _dot_f32(p8, v_refs[j * npg + r * ppb + i][...], _dot_nt)[0:1, :]
        o_ref[j] = acc / den


def sample_attn(pages, sel_flat, rel_bias, q, k_new, v_new, scores, cache_vt, layer, k_sel):
    bsz, nh, dh = q.shape
    ps = cache_vt.shape[4]
    _, nbc, _, t = scores.shape
    ppb = t // ps
    hg = HEAD_GROUP
    hps = SAMPLE_HEADS_PER_STEP
    assert nh % hg == 0 and hg % hps == 0
    npg = ppb * k_sel

    def page(j, i):
        return pl.BlockSpec((None, None, None, dh, ps),
                            lambda b, s, pg, sel: (layer, pg[(b * nh + s * hps + j) * npg + i], s * hps + j, 0, 0))

    new = lambda: pl.BlockSpec((None, hg, dh), lambda b, s, pg, sel: (b, (s * hps) // hg, 0))
    return pl.pallas_call(
        functools.partial(_sample_attn_kernel, nh=nh, k_sel=k_sel),
        grid_spec=pltpu.PrefetchScalarGridSpec(
            num_scalar_prefetch=2,
            grid=(bsz, nh // hps),
            in_specs=[pl.BlockSpec(memory_space=pltpu.SMEM), new(), new(), new(),
                      pl.BlockSpec((None, nbc, nh, t), lambda b, s, pg, sel: (b, 0, 0, 0))]
                     + [page(j, i) for j in range(hps) for i in range(npg)],
            out_specs=pl.BlockSpec((None, hps, 1, dh), lambda b, s, pg, sel: (b, s, 0, 0))),
        out_shape=jax.ShapeDtypeStruct((bsz, nh, 1, dh), F32),
        compiler_params=_params("parallel", "arbitrary"),
        name="sample_attn",
    )(pages, sel_flat, rel_bias, q, k_new, v_new, scores, *([cache_vt] * (hps * npg)))


def _mlstm_layer(hp, hs, batch, st_c, st_n, st_m, g_mix, w_in, b_gates, g_head, w_out):
    nh = ML_HEADS
    d = hp.shape[1]
    vw = w_out.shape[0]
    dv = vw // nh
    dqk = dv // 2
    qkw = nh * dqk
    w_main = jnp.concatenate([w_in[:, :qkw], w_in[:, 2 * qkw:2 * qkw + 2 * vw]], axis=1).astype(BF16)
    wkt = w_in[:, qkw:2 * qkw].T.astype(BF16)
    wgt = w_in[:, 2 * qkw + 2 * vw:].T
    w_out_b = w_out.astype(BF16)

    q, v, o, kt, gt = ml_proj(hp, g_mix, w_main, wkt, wgt, b_gates, BF16)
    hsp, c_p, n_p, m_p = mlstm_prompt(q, v, o, kt, gt, g_head, batch)

    bs = hs.shape[0]
    q, v, o, kt, gt = ml_proj(hs, g_mix, w_main, wkt, wgt, b_gates, F32)
    hss, c_s, n_s, m_s = mlstm_sample(q.reshape(bs, nh, dqk), kt.T.reshape(bs, nh, dqk), v.reshape(bs, nh, dv),
                                      o.reshape(bs, nh, dv), gt.T.reshape(bs, 1, 2 * nh), g_head,
                                      st_c, st_n, st_m.reshape(bs, 1, nh))
    return hsp, hss.reshape(bs, vw), w_out_b, (c_p, n_p, m_p), (c_s, n_s, m_s.reshape(bs, nh))


def _moba_layer(hp, hs, batch, cache_kt, cache_vt, layer, page_table, bias, kv_group, rel_bias, g_mix, w_qkv,
                w_out, kbuf, vbuf):
    _, _, nh, dh, ps = cache_kt.shape
    w = w_qkv.shape[1] // 3
    assert dh == LANES // 2 and w == nh * dh
    w_qkv_b = w_qkv.astype(BF16)
    w_out_b = w_out.astype(BF16)

    q, k, v = norm_proj(hp, g_mix, w_qkv_b, (w, w, w))
    kmeans = block_means(k)
    qa, ka, va, kbuf, vbuf = moba_select(q, k, v, kmeans, batch, layer, kbuf, vbuf, kv_group)
    o = moba_attn(qa, ka, va, bias[0], kv_group, bias[1], bias[2])

    bs, n_pages = page_table.shape
    past = n_pages * ps
    sbias = bias[3]
    assert sbias.shape == (nh, past)
    nbc = past // MOBA_BLOCK
    assert past == nbc * MOBA_BLOCK and nbc >= 1, "a partially filled tail block is not supported"
    k_sel = min(MOBA_TOPK, nbc)
    (qkv_s,) = norm_proj(hs, g_mix, w_qkv_b, (3 * w,))
    q_s, k_s, v_s = (qkv_s[:, i * w:(i + 1) * w].reshape(bs, nh, dh) for i in range(3))
    ppb = MOBA_BLOCK // ps
    scores, gsum = sample_scores(jnp.transpose(q_s, (0, 2, 1)), sbias, cache_kt, layer, page_table, nbc)
    sel = sample_select(gsum, k_sel, 1.0 / (MOBA_BLOCK * dh ** -0.5))
    sel_bhk = jnp.transpose(sel, (0, 2, 1))
    logical = sel_bhk[..., None] * ppb + jnp.arange(ppb, dtype=jnp.int32)
    pages = jnp.take_along_axis(page_table, logical.reshape(bs, -1), axis=1)
    o_s = sample_attn(pages.reshape(-1), sel_bhk.reshape(-1), rel_bias, q_s, k_s, v_s,
                      scores, cache_vt, layer, k_sel)
    return o, o_s.reshape(bs, w), w_out_b, kbuf, vbuf, (k_s, v_s)


def kernel(x_prompt, x_sample, cache_k, cache_v, state_C, state_n, state_m, page_table, norm_mix, norm_ffn,
           norm_final, rel_bias, w_ml_in, b_ml_gates, g_ml_head, w_ml_out, w_attn_qkv, w_attn_out, w_ffn_in,
           w_ffn_out):
    batch, seq, d = x_prompt.shape
    bs, dec_seq, _ = x_sample.shape
    assert dec_seq == 1 and seq % MOBA_BLOCK == 0
    depth = norm_mix.shape[0]
    nh_at = cache_k.shape[3]
    dh = cache_k.shape[4]
    hp = x_prompt.reshape(batch * seq, d)
    hs = x_sample.reshape(bs, d)
    dff = w_ffn_out.shape[1]
    assert dff % FFN_CHUNK == 0
    w_ffn_in_b = jnp.transpose(w_ffn_in.astype(BF16).reshape(depth, d, 2, dff // FFN_CHUNK, FFN_CHUNK),
                               (0, 2, 3, 1, 4))
    w_ffn_out_b = w_ffn_out.astype(BF16).reshape(depth, dff // FFN_CHUNK, FFN_CHUNK, d)
    nb = seq // MOBA_BLOCK
    kv_group = math.gcd(nb, KV_GROUP)
    d_min, d_sat = 1 - kv_group, _saturation_block_distance(MOBA_BLOCK, nb)
    bias = (bias_tiles(rel_bias, d_min, d_sat, MOBA_BLOCK), d_min, d_sat,
            sample_bias(rel_bias, page_table.shape[1] * cache_k.shape[2]))
    cache_kt = jnp.transpose(cache_k, (0, 1, 3, 4, 2))
    cache_vt = jnp.transpose(cache_v, (0, 1, 3, 4, 2))

    kbuf = jnp.zeros((depth // 2, batch, nh_at * dh, seq), F32)
    vbuf = jnp.zeros((depth // 2, batch, nh_at * dh, seq), F32)
    kv_s, st_p, st_s = [], [], []
    for layer in range(depth):
        j = layer // 2
        if layer % 2 == 0:
            ap, asm, w_o, sp, ss = _mlstm_layer(hp, hs, batch, state_C[j], state_n[j], state_m[j], norm_mix[layer],
                                                w_ml_in[j], b_ml_gates[j], g_ml_head[j], w_ml_out[j])
            st_p.append(sp)
            st_s.append(ss)
        else:
            ap, asm, w_o, kbuf, vbuf, ks = _moba_layer(hp, hs, batch, cache_kt, cache_vt, j, page_table, bias,
                                                       kv_group, rel_bias, norm_mix[layer], w_attn_qkv[j],
                                                       w_attn_out[j], kbuf, vbuf)
            kv_s.append(ks)
        g_last = norm_final if layer == depth - 1 else None
        hp = proj_ffn(ap, w_o, hp, norm_ffn[layer], w_ffn_in_b, w_ffn_out_b, layer, g_last)
        hs = proj_ffn(asm, w_o, hs, norm_ffn[layer], w_ffn_in_b, w_ffn_out_b, layer, g_last)

    y_prompt = hp.reshape(batch, seq, d)
    y_sample = hs.reshape(bs, dec_seq, d)
    stack = lambda items, i, shape: jnp.stack([it[i] for it in items]).reshape((len(items),) + shape)
    to_rows = lambda buf: jnp.transpose(buf.reshape(buf.shape[0], batch, nh_at, dh, seq), (0, 1, 4, 2, 3))
    return (y_prompt, y_sample, to_rows(kbuf), to_rows(vbuf),
            stack(kv_s, 0, (bs, dec_seq, nh_at, dh)), stack(kv_s, 1, (bs, dec_seq, nh_at, dh)),
            jnp.stack([s[0] for s in st_p]), jnp.stack([s[1] for s in st_p]), jnp.stack([s[2] for s in st_p]),
            jnp.stack([s[0] for s in st_s]), jnp.stack([s[1] for s in st_s]), jnp.stack([s[2] for s in st_s]))
```
